```python
import math
import jax, jax.numpy as jnp
from jax import lax
import numpy as np

D_MODEL = 1024
BATCH = 8
SEQ = 2048
DEPTH = 4
DEC_BATCH = 128
DEC_SEQ = 8
PAST_LEN = 2048
PAGE_SIZE = 128

N_MIXERS = 3
N_FOX_LAYERS = (DEPTH + 2) // 3
N_DSA_LAYERS = (DEPTH + 1) // 3
N_GDN_LAYERS = DEPTH // 3

BRANCH = D_MODEL
N_HEADS = 8
HEAD_DIM = BRANCH // N_HEADS
ATTN_SCALE = HEAD_DIM ** -0.5
DSA_KV_HEADS = 2
DSA_GROUP = N_HEADS // DSA_KV_HEADS
DSA_KV = DSA_KV_HEADS * HEAD_DIM
IDX_HEADS = 8
IDX_DIM = 64
IDX_SCALE = IDX_DIM ** -0.5
TOPK_MAX = 256
CONV_W = 4
GDN_CHUNK = 64
Q_BLOCK = 128
ROPE_THETA = 10000.0
EPS = 1e-6
FORGET_BIAS = 3.0
POOL_NUM = 5
POOL_DEN = 4

FOX_IN = 4 * BRANCH + N_HEADS
DSA_IN = 2 * BRANCH + 2 * DSA_KV + IDX_HEADS * IDX_DIM + IDX_HEADS + IDX_DIM
GDN_IN = 4 * BRANCH + 2 * N_HEADS

kernel_name = 'fox_dsa_gdn_hybrid_step'


def _rmsnorm(x, w):
    xf = x.astype(jnp.float32)
    y = xf * lax.rsqrt(jnp.mean(xf * xf, axis=-1, keepdims=True) + EPS)
    return (y * w.astype(jnp.float32)).astype(x.dtype)


def _heads(a, n, d):
    return a.reshape(a.shape[0], a.shape[1], n, d)


def _rope(x, pos):
    half = x.shape[-1] // 2
    inv = ROPE_THETA ** (-jnp.arange(half, dtype=jnp.float32) / half)
    ang = pos.astype(jnp.float32)[:, None] * inv[None, :]
    cos = jnp.cos(ang)[:, None, :]
    sin = jnp.sin(ang)[:, None, :]
    xf = x.astype(jnp.float32)
    x1, x2 = xf[..., :half], xf[..., half:]
    return jnp.concatenate([x1 * cos - x2 * sin, x1 * sin + x2 * cos], axis=-1).astype(x.dtype)


def _sweep_queries(fn, q_arrays, q_pos):
    lq = q_pos.shape[0]
    qb = min(Q_BLOCK, lq)
    nb = -(-lq // qb)
    pad = nb * qb - lq

    def split(a):
        a = jnp.pad(a, [(0, 0), (0, pad)] + [(0, 0)] * (a.ndim - 2))
        a = a.reshape(a.shape[0], nb, qb, *a.shape[2:])
        return jnp.moveaxis(a, 1, 0)

    pos = jnp.pad(q_pos, (0, pad), mode='edge').reshape(nb, qb)
    out = lax.map(lambda a: fn(a[0], a[1]), (tuple(split(a) for a in q_arrays), pos))
    out = jnp.moveaxis(out, 0, 1)
    return out.reshape(out.shape[0], nb * qb, *out.shape[3:])[:, :lq]


def _fox_attend(q, cum_q, k, v, cum_k, q_pos, k_pos):
    cum_k_t = jnp.swapaxes(cum_k, 1, 2)

    def block(args, pos):
        qb, cqb = args
        s = jnp.einsum('bqhd,bkhd->bhqk', qb, k).astype(jnp.float32) * ATTN_SCALE
        s = s + jnp.swapaxes(cqb, 1, 2)[..., :, None] - cum_k_t[..., None, :]
        s = jnp.where(k_pos[None, :] <= pos[:, None], s, -jnp.inf)
        p = jax.nn.softmax(s, axis=-1).astype(v.dtype)
        return jnp.einsum('bhqk,bkhd->bqhd', p, v)

    return _sweep_queries(block, (q, cum_q), q_pos)


def _fox_mixer(h, w_in, b_f, w_out, past_k, past_v, past_logf, q_pos, k_pos):
    B, L, _ = h.shape
    q, k, v, f_pre, g = jnp.split(h @ w_in, [BRANCH, 2 * BRANCH, 3 * BRANCH, 3 * BRANCH + N_HEADS], axis=-1)
    q = _heads(q, N_HEADS, HEAD_DIM)
    k = _heads(k, N_HEADS, HEAD_DIM)
    v = _heads(v, N_HEADS, HEAD_DIM)
    logf = jax.nn.log_sigmoid((f_pre + b_f).astype(jnp.float32))
    if past_k is None:
        k_all, v_all, logf_all = k, v, logf
    else:
        k_all = jnp.concatenate([past_k, k], axis=1)
        v_all = jnp.concatenate([past_v, v], axis=1)
        logf_all = jnp.concatenate([past_logf.astype(jnp.float32), logf], axis=1)
    cum = jnp.cumsum(logf_all, axis=1)
    o = _fox_attend(q, cum[:, -L:], k_all, v_all, cum, q_pos, k_pos)
    o = o.reshape(B, L, BRANCH) * jax.nn.silu(g)
    return o @ w_out, (k, v, logf.astype(h.dtype))


def _dsa_attend(q, q_idx, w_idx, k, v, k_idx, q_pos, k_pos, n_sel):
    B = q.shape[0]
    take = jax.vmap(lambda rows, idx: rows[idx])

    def block(args, pos):
        qb, qib, wib = args
        nq = qb.shape[1]
        dots = jnp.einsum('bqhd,bkd->bqhk', qib, k_idx).astype(jnp.float32) * IDX_SCALE
        score = jnp.einsum('bqh,bqhk->bqk', wib.astype(jnp.float32), jax.nn.relu(dots))
        score = jnp.where(k_pos[None, :] <= pos[:, None], score, -jnp.inf)
        _, sel = lax.top_k(score, n_sel)
        valid = k_pos[sel] <= pos[None, :, None]
        ks = take(k, sel)
        vs = take(v, sel)
        qg = qb.reshape(B, nq, DSA_KV_HEADS, DSA_GROUP, HEAD_DIM)
        s = jnp.einsum('bthgd,btnhd->bthgn', qg, ks).astype(jnp.float32) * ATTN_SCALE
        s = jnp.where(valid[:, :, None, None, :], s, -jnp.inf)
        p = jax.nn.softmax(s, axis=-1).astype(vs.dtype)
        o = jnp.einsum('bthgn,btnhd->bthgd', p, vs)
        return o.reshape(B, nq, N_HEADS, HEAD_DIM)

    return _sweep_queries(block, (q, q_idx, w_idx), q_pos)


def _dsa_mixer(h, w_in, w_out, past_k, past_v, past_kidx, q_pos, k_pos, n_sel):
    B, L, _ = h.shape
    o1 = BRANCH
    o2 = o1 + DSA_KV
    o3 = o2 + DSA_KV
    o4 = o3 + IDX_HEADS * IDX_DIM
    o5 = o4 + IDX_HEADS
    o6 = o5 + IDX_DIM
    q, k, v, qi, wi, ki, g = jnp.split(h @ w_in, [o1, o2, o3, o4, o5, o6], axis=-1)
    q = _rope(_heads(q, N_HEADS, HEAD_DIM), q_pos)
    k = _rope(_heads(k, DSA_KV_HEADS, HEAD_DIM), q_pos)
    v = _heads(v, DSA_KV_HEADS, HEAD_DIM)
    qi = _rope(_heads(qi, IDX_HEADS, IDX_DIM), q_pos)
    ki = _rope(_heads(ki, 1, IDX_DIM), q_pos)[:, :, 0]
    wi = wi * (IDX_HEADS ** -0.5)
    if past_k is None:
        k_all, v_all, ki_all = k, v, ki
    else:
        k_all = jnp.concatenate([past_k, k], axis=1)
        v_all = jnp.concatenate([past_v, v], axis=1)
        ki_all = jnp.concatenate([past_kidx, ki], axis=1)
    o = _dsa_attend(q, qi, wi, k_all, v_all, ki_all, q_pos, k_pos, n_sel)
    o = o.reshape(B, L, BRANCH) * jax.nn.silu(g)
    return o @ w_out, (k, v, ki)


def _l2norm(a):
    return a * lax.rsqrt(jnp.sum(a * a, axis=-1, keepdims=True) + EPS)


def _gated_delta_chunked(q, k, v, g, beta, S0):
    B, L, H, _ = q.shape
    C = min(GDN_CHUNK, L)
    nc = -(-L // C)
    pad = nc * C - L

    def to_chunks(a):
        a = jnp.pad(a, [(0, 0), (0, pad)] + [(0, 0)] * (a.ndim - 2))
        a = a.reshape(B, nc, C, *a.shape[2:])
        return jnp.moveaxis(jnp.moveaxis(a, 1, 0), 3, 2)

    q, k, v, g, beta = (to_chunks(a) for a in (q, k, v, g, beta))
    G = jnp.cumsum(g, axis=-1)
    incl = jnp.tril(jnp.ones((C, C), dtype=bool))
    strict = jnp.tril(jnp.ones((C, C), dtype=bool), -1)
    decay = jnp.exp(jnp.where(incl, G[..., :, None] - G[..., None, :], -jnp.inf))
    kk = jnp.einsum('...id,...jd->...ij', k, k)
    A = jnp.where(strict, beta[..., :, None] * kk * decay, 0.0)
    IA = A + jnp.eye(C, dtype=A.dtype)
    W = lax.linalg.triangular_solve(IA, (beta * jnp.exp(G))[..., None] * k, left_side=True, lower=True, unit_diagonal=True)
    U = lax.linalg.triangular_solve(IA, beta[..., None] * v, left_side=True, lower=True, unit_diagonal=True)
    Aqk = jnp.einsum('...id,...jd->...ij', q, k) * decay
    k_dec = k * jnp.exp(G[..., -1:] - G)[..., None]
    g_end = jnp.exp(G[..., -1])

    def step(S, xs):
        Wc, Uc, qc, Aqkc, kdc, Gc, gec = xs
        Vn = Uc - Wc @ S
        o = jnp.exp(Gc)[..., None] * (qc @ S) + Aqkc @ Vn
        S = gec[..., None, None] * S + jnp.swapaxes(kdc, -1, -2) @ Vn
        return S, o

    S, o = lax.scan(step, S0, (W, U, q, Aqk, k_dec, G, g_end))
    o = jnp.transpose(o, (1, 0, 3, 2, 4)).reshape(B, nc * C, H, o.shape[-1])[:, :L]
    return o, S


def _gdn_mixer(h, w_in, conv_w, A_log, dt_bias, norm_w, w_out, conv_state, S0):
    B, L, _ = h.shape
    qkv, a, b, z = jnp.split(h @ w_in, [3 * BRANCH, 3 * BRANCH + N_HEADS, 3 * BRANCH + 2 * N_HEADS], axis=-1)
    buf = jnp.concatenate([conv_state.astype(qkv.dtype), qkv], axis=1)
    conv = sum(buf[:, j:j + L] * conv_w[j] for j in range(CONV_W))
    new_conv = buf[:, -(CONV_W - 1):]
    conv = jax.nn.silu(conv).astype(jnp.float32)
    q, k, v = jnp.split(conv, 3, axis=-1)
    q = _l2norm(_heads(q, N_HEADS, HEAD_DIM)) * ATTN_SCALE
    k = _l2norm(_heads(k, N_HEADS, HEAD_DIM))
    v = _heads(v, N_HEADS, HEAD_DIM)
    g = -jnp.exp(A_log.astype(jnp.float32)) * jax.nn.softplus(a.astype(jnp.float32) + dt_bias.astype(jnp.float32))
    beta = jax.nn.sigmoid(b.astype(jnp.float32))
    o, S = _gated_delta_chunked(q, k, v, g, beta, S0.astype(jnp.float32))
    o = _rmsnorm(o, norm_w).astype(h.dtype) * jax.nn.silu(_heads(z, N_HEADS, HEAD_DIM))
    return o.reshape(B, L, BRANCH) @ w_out, (new_conv.astype(h.dtype), S.astype(h.dtype))


def setup_inputs(seed: int = 0) -> dict:
    key = jax.random.key(seed)
    ks = jax.random.split(key, 24)
    n_pages = PAST_LEN // PAGE_SIZE
    n_used = DEC_BATCH * n_pages
    n_pool = (n_used * POOL_NUM) // POOL_DEN
    f32 = jnp.float32

    def nrm(k, shape, s=1.0):
        return s * jax.random.normal(k, shape, f32)

    x_prompt = nrm(ks[0], (BATCH, SEQ, D_MODEL))
    x_sample = nrm(ks[1], (DEC_BATCH, DEC_SEQ, D_MODEL))
    cache_fox_k = nrm(ks[2], (N_FOX_LAYERS, n_pool, PAGE_SIZE, N_HEADS, HEAD_DIM))
    cache_fox_v = nrm(ks[3], (N_FOX_LAYERS, n_pool, PAGE_SIZE, N_HEADS, HEAD_DIM))
    cache_fox_logf = jax.nn.log_sigmoid(FORGET_BIAS + nrm(ks[4], (N_FOX_LAYERS, n_pool, PAGE_SIZE, N_HEADS), 1.0))
    cache_dsa_k = nrm(ks[5], (N_DSA_LAYERS, n_pool, PAGE_SIZE, DSA_KV_HEADS, HEAD_DIM))
    cache_dsa_v = nrm(ks[6], (N_DSA_LAYERS, n_pool, PAGE_SIZE, DSA_KV_HEADS, HEAD_DIM))
    cache_dsa_kidx = nrm(ks[7], (N_DSA_LAYERS, n_pool, PAGE_SIZE, IDX_DIM))
    state_gdn_conv = nrm(ks[8], (N_GDN_LAYERS, DEC_BATCH, CONV_W - 1, 3 * BRANCH))
    state_gdn_S = nrm(ks[9], (N_GDN_LAYERS, DEC_BATCH, N_HEADS, HEAD_DIM, HEAD_DIM), 0.1)
    page_table = jax.random.permutation(ks[10], n_pool)[:n_used].reshape(DEC_BATCH, n_pages).astype(jnp.int32)
    norm_w = 1.0 + nrm(ks[11], (DEPTH, D_MODEL), 0.02)
    norm_f = 1.0 + nrm(ks[12], (D_MODEL,), 0.02)
    fox_w_in = nrm(ks[13], (N_FOX_LAYERS, D_MODEL, FOX_IN), D_MODEL ** -0.5)
    fox_b_f = FORGET_BIAS + nrm(ks[14], (N_FOX_LAYERS, N_HEADS), 0.5)
    fox_w_out = nrm(ks[15], (N_FOX_LAYERS, BRANCH, D_MODEL), BRANCH ** -0.5)
    dsa_w_in = nrm(ks[16], (N_DSA_LAYERS, D_MODEL, DSA_IN), D_MODEL ** -0.5)
    dsa_w_out = nrm(ks[17], (N_DSA_LAYERS, BRANCH, D_MODEL), BRANCH ** -0.5)
    gdn_w_in = nrm(ks[18], (N_GDN_LAYERS, D_MODEL, GDN_IN), D_MODEL ** -0.5)
    gdn_conv_w = nrm(ks[19], (N_GDN_LAYERS, CONV_W, 3 * BRANCH), CONV_W ** -0.5)
    gdn_A_log = jnp.log(jax.random.uniform(ks[20], (N_GDN_LAYERS, N_HEADS), f32, 1.0, 16.0))
    dt = jnp.exp(jax.random.uniform(ks[21], (N_GDN_LAYERS, N_HEADS), f32, math.log(1e-3), math.log(1e-1)))
    gdn_dt_bias = dt + jnp.log(-jnp.expm1(-dt))
    gdn_norm_w = 1.0 + nrm(ks[22], (N_GDN_LAYERS, HEAD_DIM), 0.02)
    gdn_w_out = nrm(ks[23], (N_GDN_LAYERS, BRANCH, D_MODEL), BRANCH ** -0.5)
    return {'x_prompt': x_prompt, 'x_sample': x_sample,
            'cache_fox_k': cache_fox_k, 'cache_fox_v': cache_fox_v, 'cache_fox_logf': cache_fox_logf,
            'cache_dsa_k': cache_dsa_k, 'cache_dsa_v': cache_dsa_v, 'cache_dsa_kidx': cache_dsa_kidx,
            'state_gdn_conv': state_gdn_conv, 'state_gdn_S': state_gdn_S, 'page_table': page_table,
            'norm_w': norm_w, 'norm_f': norm_f,
            'fox_w_in': fox_w_in, 'fox_b_f': fox_b_f, 'fox_w_out': fox_w_out,
            'dsa_w_in': dsa_w_in, 'dsa_w_out': dsa_w_out,
            'gdn_w_in': gdn_w_in, 'gdn_conv_w': gdn_conv_w, 'gdn_A_log': gdn_A_log,
            'gdn_dt_bias': gdn_dt_bias, 'gdn_norm_w': gdn_norm_w, 'gdn_w_out': gdn_w_out}


def reference(x_prompt, x_sample, cache_fox_k, cache_fox_v, cache_fox_logf, cache_dsa_k, cache_dsa_v,
              cache_dsa_kidx, state_gdn_conv, state_gdn_S, page_table, norm_w, norm_f,
              fox_w_in, fox_b_f, fox_w_out, dsa_w_in, dsa_w_out,
              gdn_w_in, gdn_conv_w, gdn_A_log, gdn_dt_bias, gdn_norm_w, gdn_w_out):
    pos_p = jnp.arange(SEQ)
    pos_s = PAST_LEN + jnp.arange(DEC_SEQ)
    kpos_s = jnp.arange(PAST_LEN + DEC_SEQ)
    sel_p = min(TOPK_MAX, SEQ // 4)
    sel_s = min(TOPK_MAX, (PAST_LEN + DEC_SEQ) // 4)

    def paged(pool, j):
        rows = pool[j, page_table]
        return rows.reshape(DEC_BATCH, PAST_LEN, *rows.shape[3:])

    fox_p, fox_s, dsa_p, dsa_s, gdn_p, gdn_s = [], [], [], [], [], []
    xp, xs = x_prompt, x_sample
    for i in range(DEPTH):
        kind, j = i % N_MIXERS, i // N_MIXERS
        hp = _rmsnorm(xp, norm_w[i])
        hs = _rmsnorm(xs, norm_w[i])
        if kind == 0:
            yp, st_p = _fox_mixer(hp, fox_w_in[j], fox_b_f[j], fox_w_out[j], None, None, None, pos_p, pos_p)
            ys, st_s = _fox_mixer(hs, fox_w_in[j], fox_b_f[j], fox_w_out[j], paged(cache_fox_k, j),
                                  paged(cache_fox_v, j), paged(cache_fox_logf, j), pos_s, kpos_s)
            fox_p.append(st_p)
            fox_s.append(st_s)
        elif kind == 1:
            yp, st_p = _dsa_mixer(hp, dsa_w_in[j], dsa_w_out[j], None, None, None, pos_p, pos_p, sel_p)
            ys, st_s = _dsa_mixer(hs, dsa_w_in[j], dsa_w_out[j], paged(cache_dsa_k, j), paged(cache_dsa_v, j),
                                  paged(cache_dsa_kidx, j), pos_s, kpos_s, sel_s)
            dsa_p.append(st_p)
            dsa_s.append(st_s)
        else:
            conv0 = jnp.zeros((BATCH, CONV_W - 1, 3 * BRANCH), hp.dtype)
            S0 = jnp.zeros((BATCH, N_HEADS, HEAD_DIM, HEAD_DIM), jnp.float32)
            yp, st_p = _gdn_mixer(hp, gdn_w_in[j], gdn_conv_w[j], gdn_A_log[j], gdn_dt_bias[j], gdn_norm_w[j],
                                  gdn_w_out[j], conv0, S0)
            ys, st_s = _gdn_mixer(hs, gdn_w_in[j], gdn_conv_w[j], gdn_A_log[j], gdn_dt_bias[j], gdn_norm_w[j],
                                  gdn_w_out[j], state_gdn_conv[j], state_gdn_S[j])
            gdn_p.append(st_p)
            gdn_s.append(st_s)
        xp = xp + yp
        xs = xs + ys

    def stk(lst, n):
        return jnp.stack([t[n] for t in lst], axis=0)

    y_prompt = _rmsnorm(xp, norm_f)
    y_sample = _rmsnorm(xs, norm_f)
    return (y_prompt, y_sample,
            stk(fox_p, 0), stk(fox_p, 1), stk(fox_p, 2),
            stk(fox_s, 0), stk(fox_s, 1), stk(fox_s, 2),
            stk(dsa_p, 0), stk(dsa_p, 1), stk(dsa_p, 2),
            stk(dsa_s, 0), stk(dsa_s, 1), stk(dsa_s, 2),
            stk(gdn_p, 0), stk(gdn_p, 1),
            stk(gdn_s, 0), stk(gdn_s, 1))
```

```python
import functools
import math

import jax
import jax.numpy as jnp
from jax import lax
from jax.experimental import pallas as pl
from jax.experimental.pallas import tpu as pltpu

F32 = jnp.float32
BF16 = jnp.bfloat16
HIGHEST = lax.Precision.HIGHEST

D_MODEL = 1024
N_HEADS = 8
HEAD_DIM = 128
BRANCH = N_HEADS * HEAD_DIM
ATTN_SCALE = HEAD_DIM ** -0.5
DSA_KV_HEADS = 2
DSA_GROUP = N_HEADS // DSA_KV_HEADS
DSA_KV = DSA_KV_HEADS * HEAD_DIM
IDX_HEADS = 8
IDX_DIM = 64
IDX_SCALE = IDX_DIM ** -0.5
TOPK_MAX = 256
CONV_W = 4
GDN_CHUNK = 64
ROPE_THETA = 10000.0
EPS = 1e-6
PAGE = 128
LANES = 128
SUBLANES = 8
NEG = -1e30
INT_MIN = -(2 ** 31)
NEG_INF_KEY = INT_MIN + 0x007FFFFF
VMEM_LIMIT = 56 * 1024 * 1024


def _dot(a, b, prec=None):
    return jnp.dot(a, b, preferred_element_type=F32, precision=prec)


def _dot_nt(a, b, prec=None):
    return lax.dot_general(a, b, (((1,), (1,)), ((), ())), preferred_element_type=F32, precision=prec)


def _dot_tn(a, b, prec=None):
    return lax.dot_general(a, b, (((0,), (0,)), ((), ())), preferred_element_type=F32, precision=prec)


def _dot1(a, b, dot=_dot):
    return dot(a.astype(BF16), b.astype(BF16))


def _iota(shape, axis):
    return lax.broadcasted_iota(jnp.int32, shape, axis)


def _tril(n, strict=False):
    r, c = _iota((n, n), 0), _iota((n, n), 1)
    return (r > c) if strict else (r >= c)


def _sigmoid(x):
    return 1.0 / (1.0 + jnp.exp(-x))


def _silu(x):
    return x * _sigmoid(x)


def _softplus(x):
    return jnp.maximum(x, 0.0) + jnp.log1p(jnp.exp(-jnp.abs(x)))


def _params(*sem):
    return pltpu.CompilerParams(dimension_semantics=sem, vmem_limit_bytes=VMEM_LIMIT)


def _head_rows(h, n_tok, n_heads):
    return pl.ds(h, n_tok, stride=n_heads)


def _normed(x_ref, nw_ref):
    x = x_ref[...]
    ms = jnp.mean(x * x, axis=-1, keepdims=True)
    return (x * lax.rsqrt(ms + EPS) * nw_ref[...]).astype(BF16)


def _fox_in_kernel(x_ref, nw_ref, w_ref, ws_ref, bf_ref, q_ref, k_ref, v_ref, g_ref, lf_ref, *copies):
    tm = x_ref.shape[0]
    h = _normed(x_ref, nw_ref)
    q_ref[...] = (_dot(h, w_ref[:, 0:BRANCH]) * ATTN_SCALE).astype(BF16)
    for j, ref in ((1, k_ref), (2, v_ref)):
        for hh in range(N_HEADS):
            col = j * BRANCH + hh * HEAD_DIM
            r = _dot(h, w_ref[:, col:col + HEAD_DIM])
            ref[_head_rows(hh, tm, N_HEADS), :] = r
            if copies:
                copies[j - 1][:, hh * HEAD_DIM:(hh + 1) * HEAD_DIM] = r.astype(BF16)
    g_ref[...] = _dot(h, w_ref[:, 3 * BRANCH:4 * BRANCH]).astype(BF16)
    f = _dot(h, ws_ref[...])[:, 0:N_HEADS] + bf_ref[...]
    lf_ref[...] = -_softplus(-f)


def _rope128(x, cos, sin_signed):
    return x * cos + pltpu.roll(x, HEAD_DIM // 2, 1) * sin_signed


def _rope64(x, cos, sin_signed):
    lane = _iota(x.shape, 1)
    rot = jnp.where((lane & (IDX_DIM - 1)) < IDX_DIM // 2,
                    pltpu.roll(x, LANES - IDX_DIM // 2, 1), pltpu.roll(x, IDX_DIM // 2, 1))
    return x * cos + rot * sin_signed


def _dsa_in_kernel(x_ref, nw_ref, w_ref, ws_ref, c128_ref, s128_ref, c64_ref, s64_ref,
                   q_ref, k_ref, v_ref, qi_ref, g_ref, ki_ref, wi_ref, *copies):
    tm = x_ref.shape[0]
    h = _normed(x_ref, nw_ref)
    c128, s128, c64, s64 = c128_ref[...], s128_ref[...], c64_ref[...], s64_ref[...]
    o_k, o_v, o_qi, o_g = BRANCH, BRANCH + DSA_KV, BRANCH + 2 * DSA_KV, BRANCH + 2 * DSA_KV + IDX_HEADS * IDX_DIM
    for hh in range(N_HEADS):
        sl = slice(hh * HEAD_DIM, (hh + 1) * HEAD_DIM)
        q_ref[:, sl] = (_rope128(_dot(h, w_ref[:, sl]), c128, s128) * ATTN_SCALE).astype(BF16)
    for hh in range(DSA_KV_HEADS):
        sl = slice(hh * HEAD_DIM, (hh + 1) * HEAD_DIM)
        kk = _rope128(_dot(h, w_ref[:, o_k + hh * HEAD_DIM:o_k + (hh + 1) * HEAD_DIM]), c128, s128)
        vv = _dot(h, w_ref[:, o_v + hh * HEAD_DIM:o_v + (hh + 1) * HEAD_DIM])
        k_ref[_head_rows(hh, tm, DSA_KV_HEADS), :] = kk
        v_ref[_head_rows(hh, tm, DSA_KV_HEADS), :] = vv
        if copies:
            copies[0][:, sl] = kk.astype(BF16)
            copies[1][:, sl] = vv.astype(BF16)
    for hh in range(IDX_HEADS * IDX_DIM // LANES):
        sl = slice(hh * LANES, (hh + 1) * LANES)
        qi_ref[:, sl] = _rope64(_dot(h, w_ref[:, o_qi + hh * LANES:o_qi + (hh + 1) * LANES]), c64, s64).astype(BF16)
    g_ref[...] = _dot(h, w_ref[:, o_g:o_g + BRANCH]).astype(BF16)
    small = _dot(h, ws_ref[...])
    ki_ref[...] = _rope64(small, c64, s64)[:, 0:IDX_DIM]
    wi_ref[...] = small[:, IDX_DIM:IDX_DIM + IDX_HEADS] * (IDX_HEADS ** -0.5 * IDX_SCALE)


def _gdn_in_kernel(x_ref, nw_ref, w_ref, ws_ref, alog_ref, dtb_ref, qkv_ref, z_ref, g_ref, beta_ref):
    h = _normed(x_ref, nw_ref)
    for j in range(3):
        sl = slice(j * BRANCH, (j + 1) * BRANCH)
        qkv_ref[:, sl] = _dot(h, w_ref[:, sl])
    z_ref[...] = _dot(h, w_ref[:, 3 * BRANCH:4 * BRANCH]).astype(BF16)
    small = _dot(h, ws_ref[...])
    a = small[:, 0:N_HEADS]
    b = small[:, N_HEADS:2 * N_HEADS]
    g_ref[...] = -jnp.exp(alog_ref[...]) * _softplus(a + dtb_ref[...])
    beta_ref[...] = _sigmoid(b)


def _row_spec(tm, n):
    return pl.BlockSpec((tm, n), lambda i: (i, 0))


def _full_spec(shape):
    return pl.BlockSpec(shape, lambda i: (0,) * len(shape))


def _in_proj(kern, x2d, consts, tables, outs, tm):
    T = x2d.shape[0]
    in_specs = [_row_spec(tm, D_MODEL)] + [_full_spec(c.shape) for c in consts[:3]]
    args = [x2d] + list(consts[:3])
    for t in tables:
        nblk = t.shape[0] // tm
        in_specs.append(pl.BlockSpec((tm, LANES), lambda i, nblk=nblk: (i % nblk, 0)))
        args.append(t)
    for c in consts[3:]:
        in_specs.append(_full_spec(c.shape))
        args.append(c)
    return pl.pallas_call(
        kern,
        grid=(T // tm,),
        in_specs=in_specs,
        out_specs=[_row_spec(tm * r, n) for r, n, _ in outs],
        out_shape=[jax.ShapeDtypeStruct((T * r, n), dt) for r, n, dt in outs],
        compiler_params=_params("parallel"),
    )(*args)


def _rope_tables(pos, head_dim, tm):
    half = head_dim // 2
    inv = ROPE_THETA ** (-jnp.arange(half, dtype=F32) / half)
    ang = pos.astype(F32)[:, None] * inv[None, :]
    cos, sin = jnp.cos(ang), jnp.sin(ang)
    reps = LANES // head_dim
    cf = jnp.tile(jnp.concatenate([cos, cos], axis=-1), (1, reps))
    ss = jnp.tile(jnp.concatenate([-sin, sin], axis=-1), (1, reps))
    if cf.shape[0] < tm:
        cf = jnp.tile(cf, (tm // cf.shape[0], 1))
        ss = jnp.tile(ss, (tm // ss.shape[0], 1))
    return cf, ss


def _out_kernel(*refs, gated, final):
    refs = list(refs)
    o_ref = refs.pop(0)
    g_ref = refs.pop(0) if gated else None
    x_ref, w_ref = refs.pop(0), refs.pop(0)
    nf_ref = refs.pop(0) if final else None
    y_ref = refs.pop(0)
    o = o_ref[...]
    if gated:
        o = (o.astype(F32) * _silu(g_ref[...].astype(F32))).astype(BF16)
    y = x_ref[...] + _dot(o, w_ref[...])
    if final:
        ms = jnp.mean(y * y, axis=-1, keepdims=True)
        y = y * lax.rsqrt(ms + EPS) * nf_ref[...]
    y_ref[...] = y


def _out_proj(o, g, x2d, w_bf16, norm_f, tm):
    T = x2d.shape[0]
    gated, final = g is not None, norm_f is not None
    args, specs = [o], [_row_spec(tm, BRANCH)]
    if gated:
        args.append(g)
        specs.append(_row_spec(tm, BRANCH))
    args += [x2d, w_bf16]
    specs += [_row_spec(tm, D_MODEL), _full_spec(w_bf16.shape)]
    if final:
        args.append(norm_f.reshape(1, D_MODEL))
        specs.append(_full_spec((1, D_MODEL)))
    return pl.pallas_call(
        functools.partial(_out_kernel, gated=gated, final=final),
        grid=(T // tm,), in_specs=specs, out_specs=_row_spec(tm, D_MODEL),
        out_shape=jax.ShapeDtypeStruct((T, D_MODEL), F32),
        compiler_params=_params("parallel"),
    )(*args)


def _cumsum_kernel(x_ref, o_ref, *, n_blk):
    tril = _tril(PAGE).astype(F32)
    carry = jnp.zeros((1, N_HEADS), F32)
    for c in range(n_blk):
        y = _dot(tril, x_ref[0, c * PAGE:(c + 1) * PAGE, :], HIGHEST) + carry
        o_ref[0, c * PAGE:(c + 1) * PAGE, :] = y
        carry = y[PAGE - 1:PAGE, :]


def _cumsum_seq(logf):
    B, L, H = logf.shape
    return pl.pallas_call(
        functools.partial(_cumsum_kernel, n_blk=L // PAGE),
        grid=(B,),
        in_specs=[pl.BlockSpec((1, L, H), lambda b: (b, 0, 0))],
        out_specs=pl.BlockSpec((1, L, H), lambda b: (b, 0, 0)),
        out_shape=jax.ShapeDtypeStruct((B, L, H), F32),
        compiler_params=_params("parallel"),
    )(logf)


def _softmax_step(s, m, l, acc, pv_fn):
    m_new = jnp.maximum(m, jnp.max(s, axis=-1, keepdims=True))
    alpha = jnp.exp(m - m_new)
    p = jnp.exp(s - m_new)
    l = alpha * l + jnp.sum(p, axis=-1, keepdims=True)
    acc = alpha * acc + pv_fn(p.astype(BF16))
    return m_new, l, acc


def _fox_attn_kernel(q_ref, k_ref, v_ref, ck_ref, o_ref, *, tq):
    i = pl.program_id(2)
    n_sub = 2 if tq % 256 == 0 else 1
    ts = tq // n_sub
    qs = [q_ref[0, r * ts:(r + 1) * ts, :] for r in range(n_sub)]

    def step(j, carry, masked):
        start = pl.multiple_of(j * tq, tq)
        k = k_ref[0, pl.ds(start, tq), :]
        v = v_ref[0, pl.ds(start, tq), :]
        ck = ck_ref[0, 0, j]
        out = []
        for r in range(n_sub):
            s = _dot_nt(qs[r], k) - ck
            if masked:
                s = jnp.where(_iota((ts, tq), 1) <= r * ts + _iota((ts, tq), 0), s, NEG)
            out.append(_softmax_step(s, *carry[r], lambda p: _dot(p, v)))
        return tuple(out)

    init = tuple((jnp.full((ts, 1), NEG, F32), jnp.zeros((ts, 1), F32), jnp.zeros((ts, HEAD_DIM), F32))
                 for _ in range(n_sub))
    carry = lax.fori_loop(0, i, lambda j, c: step(j, c, False), init)
    for r, (_, l, acc) in enumerate(step(i, carry, True)):
        o_ref[0, r * ts:(r + 1) * ts, :] = (acc / l).astype(BF16)


def _fox_attn_prompt(q, k, v, cum, tq):
    B, L, _ = q.shape
    nb = L // tq
    ck = jnp.swapaxes(cum, 1, 2).reshape(B, N_HEADS, nb, 1, tq)
    return pl.pallas_call(
        functools.partial(_fox_attn_kernel, tq=tq),
        grid=(B, N_HEADS, nb),
        in_specs=[
            pl.BlockSpec((1, tq, HEAD_DIM), lambda b, h, i: (b, i, h)),
            pl.BlockSpec((1, L, HEAD_DIM), lambda b, h, i: (b, 0, h)),
            pl.BlockSpec((1, L, HEAD_DIM), lambda b, h, i: (b, 0, h)),
            pl.BlockSpec((1, 1, nb, 1, tq), lambda b, h, i: (b, h, 0, 0, 0)),
        ],
        out_specs=pl.BlockSpec((1, tq, HEAD_DIM), lambda b, h, i: (b, i, h)),
        out_shape=jax.ShapeDtypeStruct((B, L, BRANCH), BF16),
        compiler_params=_params("parallel", "parallel", "arbitrary"),
    )(q, k, v, ck)


def _cumsum_lanes(x):
    lane = _iota(x.shape, 1)
    d = 1
    while d < x.shape[1]:
        x = x + jnp.where(lane >= d, pltpu.roll(x, d, 1), 0.0)
        d *= 2
    return x


def _fox_dec_kernel(pt_ref, q_ref, kn_ref, vn_ref, lfn_ref, *rest, pps, n_steps, dl):
    k_refs, v_refs, lf_refs = rest[0:pps], rest[pps:2 * pps], rest[2 * pps:3 * pps]
    o_ref = rest[3 * pps]
    m_ref, l_ref, acc_ref, car_ref = rest[3 * pps + 1:]
    p = pl.program_id(1)
    nr = N_HEADS * dl

    @pl.when(p == 0)
    def _():
        m_ref[...] = jnp.full((nr, 1), NEG, F32)
        l_ref[...] = jnp.zeros((nr, 1), F32)
        acc_ref[...] = jnp.zeros((nr, HEAD_DIM), F32)
        car_ref[...] = jnp.zeros((N_HEADS, LANES), F32)

    qs = [q_ref[0, :, h * HEAD_DIM:(h + 1) * HEAD_DIM] for h in range(N_HEADS)]

    def attend(blocks, n_tok, mask):
        local = [_cumsum_lanes(lf) for _, _, lf in blocks]
        cums, off = [], car_ref[...]
        for c in local:
            cums.append(c + off)
            off = off + jnp.broadcast_to(c[:, LANES - 1:LANES], (N_HEADS, LANES))
        car_ref[...] = off
        cum = jnp.concatenate([c[:, 0:n_tok] for c in cums], axis=1)

        def head_rows(which, h):
            return jnp.concatenate([blk[which][_head_rows(h, n_tok, N_HEADS), :].astype(BF16) for blk in blocks],
                                   axis=0)

        s = jnp.concatenate([_dot_nt(qs[h], head_rows(0, h)) - cum[h:h + 1, :] for h in range(N_HEADS)], axis=0)
        if mask is not None:
            s = jnp.where(mask, s, NEG)

        def pv(pb):
            return jnp.concatenate([_dot(pb[h * dl:(h + 1) * dl], head_rows(1, h)) for h in range(N_HEADS)], axis=0)

        m, l, acc = _softmax_step(s, m_ref[...], l_ref[...], acc_ref[...], pv)
        m_ref[...], l_ref[...], acc_ref[...] = m, l, acc

    attend([(k_refs[r], v_refs[r], lf_refs[r][...]) for r in range(pps)], PAGE, None)

    @pl.when(p == n_steps - 1)
    def _():
        causal = _iota((nr, dl), 1) <= _iota((nr, dl), 0) % dl
        attend([(kn_ref, vn_ref, lfn_ref[0])], dl, causal)
        out = acc_ref[...] / l_ref[...]
        for h in range(N_HEADS):
            o_ref[0, :, h * HEAD_DIM:(h + 1) * HEAD_DIM] = out[h * dl:(h + 1) * dl].astype(BF16)


def _fox_attn_decode(q, kn, vn, lfn, cache_k, cache_v, cache_lf, layer, pt_flat, n_pages, pps):
    DB, DL, _ = q.shape
    n_steps = n_pages // pps
    nr = N_HEADS * DL
    lfn_t = jnp.pad(jnp.swapaxes(lfn, 1, 2), ((0, 0), (0, 0), (0, LANES - DL)))

    def page_spec(rows, r):
        return pl.BlockSpec((None, None, rows, LANES),
                            lambda b, p, pt, r=r: (layer, pt[b * n_pages + p * pps + r], 0, 0))

    new_rows = pl.BlockSpec((DL * N_HEADS, HEAD_DIM), lambda b, p, pt: (b, 0))
    tok = lambda n, w: pl.BlockSpec((1, n, w), lambda b, p, pt: (b, 0, 0))
    in_specs = [tok(DL, BRANCH), new_rows, new_rows, tok(N_HEADS, LANES)]
    in_specs += [page_spec(PAGE * N_HEADS, r) for r in range(pps)] * 2 + [page_spec(N_HEADS, r) for r in range(pps)]
    grid_spec = pltpu.PrefetchScalarGridSpec(
        num_scalar_prefetch=1, grid=(DB, n_steps), in_specs=in_specs, out_specs=tok(DL, BRANCH),
        scratch_shapes=[pltpu.VMEM((nr, 1), F32), pltpu.VMEM((nr, 1), F32), pltpu.VMEM((nr, HEAD_DIM), F32),
                        pltpu.VMEM((N_HEADS, LANES), F32)])
    return pl.pallas_call(
        functools.partial(_fox_dec_kernel, pps=pps, n_steps=n_steps, dl=DL),
        grid_spec=grid_spec,
        out_shape=jax.ShapeDtypeStruct((DB, DL, BRANCH), BF16),
        compiler_params=_params("parallel", "arbitrary"),
    )(pt_flat, q, kn, vn, lfn_t, *([cache_k] * pps), *([cache_v] * pps), *([cache_lf] * pps))


def _key_to_float(t):
    return pltpu.bitcast(t ^ ((t >> 31) & 0x7FFFFFFF), F32)


def _kth_largest(count_ge, shape, n_sel, bits_per_step):
    def step(si, t):
        shift = 32 - bits_per_step * (si + 1)
        best = t
        for j in range(1, 2 ** bits_per_step):
            cand = t ^ lax.shift_left(jnp.int32(j), shift)
            best = jnp.where(count_ge(_key_to_float(cand)) >= n_sel, jnp.maximum(best, cand), best)
        return best
    t = lax.fori_loop(0, 32 // bits_per_step, step, jnp.full(shape, INT_MIN, jnp.int32))
    return _key_to_float(jnp.maximum(t, NEG_INF_KEY + 1))


def _strict_upper(n):
    return (_iota((n, n), 0) < _iota((n, n), 1)).astype(BF16)


def _dsa_attn_kernel(q_ref, qi_ref, wi_ref, k_ref, v_ref, ki_ref, o_ref, sc_ref, wb_ref, *, n_sel, tq, W):
    i = pl.program_id(1)
    nch = (i * tq + tq + W - 1) // W
    reps = W // LANES
    wi = wi_ref[0]
    for h in range(IDX_HEADS):
        wb_ref[h] = jnp.broadcast_to(wi[:, h:h + 1], (tq, LANES))
    row, col = _iota((tq, W), 0), _iota((tq, W), 1)

    def fill(c, _):
        kc = ki_ref[0, pl.ds(pl.multiple_of(c * W, W), W), :].astype(BF16)
        sc = jnp.zeros((tq, W), F32)
        for h in range(IDX_HEADS):
            d = _dot_nt(qi_ref[0, :, h * IDX_DIM:(h + 1) * IDX_DIM], kc)
            sc = sc + jnp.concatenate([wb_ref[h]] * reps, axis=1) * jnp.maximum(d, 0.0)
        sc_ref[c] = jnp.where(c * W + col <= i * tq + row, sc, -jnp.inf)
        return 0

    lax.fori_loop(0, nch, fill, 0)

    def count(pred):
        def body(c, acc):
            x = jnp.where(pred(sc_ref[c]), 1.0, 0.0)
            for r in range(reps):
                acc = acc + x[:, r * LANES:(r + 1) * LANES]
            return acc
        return jnp.sum(lax.fori_loop(0, nch, body, jnp.zeros((tq, LANES), F32)), axis=-1, keepdims=True)

    thr = _kth_largest(lambda f: count(lambda s: s >= f), (tq, 1), n_sel, 1)

    @pl.when(jnp.max(count(lambda s: s >= thr)) > n_sel)
    def _():
        need = n_sel - count(lambda s: s > thr)
        upper = _strict_upper(W)

        def body(c, seen):
            sc = sc_ref[c]
            eq = sc == thr
            eqf = jnp.where(eq, 1.0, 0.0)
            rank = seen + _dot(eqf.astype(BF16), upper)
            sc_ref[c] = jnp.where(eq & (rank >= need), -jnp.inf, sc)
            return seen + jnp.sum(eqf, axis=-1, keepdims=True)

        lax.fori_loop(0, nch, body, jnp.zeros((tq, 1), F32))

    qgs = [jnp.concatenate(
        [q_ref[0, :, (g * DSA_GROUP + r) * HEAD_DIM:(g * DSA_GROUP + r + 1) * HEAD_DIM] for r in range(DSA_GROUP)],
        axis=0) for g in range(DSA_KV_HEADS)]

    def body(c, carry):
        start = pl.multiple_of(c * W, W)
        sel = (sc_ref[c] >= thr)[None]
        out = []
        for g in range(DSA_KV_HEADS):
            m, l, acc = carry[g]
            gsl = slice(g * HEAD_DIM, (g + 1) * HEAD_DIM)
            s = _dot_nt(qgs[g], k_ref[0, pl.ds(start, W), gsl]).reshape(DSA_GROUP, tq, W)
            s = jnp.where(sel, s, NEG)
            m_new = jnp.maximum(m, jnp.max(s, axis=-1, keepdims=True))
            alpha = jnp.exp(m - m_new)
            p = jnp.exp(s - m_new)
            l = alpha * l + jnp.sum(p, axis=-1, keepdims=True)
            pv = _dot(p.reshape(DSA_GROUP * tq, W).astype(BF16), v_ref[0, pl.ds(start, W), gsl])
            out.append((m_new, l, alpha * acc + pv.reshape(DSA_GROUP, tq, HEAD_DIM)))
        return tuple(out)

    init = tuple((jnp.full((DSA_GROUP, tq, 1), NEG, F32), jnp.zeros((DSA_GROUP, tq, 1), F32),
                  jnp.zeros((DSA_GROUP, tq, HEAD_DIM), F32)) for _ in range(DSA_KV_HEADS))
    res = lax.fori_loop(0, nch, body, init)
    for g, (_, l, acc) in enumerate(res):
        out = acc / l
        for r in range(DSA_GROUP):
            hh = g * DSA_GROUP + r
            o_ref[0, :, hh * HEAD_DIM:(hh + 1) * HEAD_DIM] = out[r].astype(BF16)


def _dsa_attn_prompt(q, qi, wi, k, v, ki, n_sel, tq):
    B, L, _ = q.shape
    nb = L // tq
    W = min(4 * LANES, L)
    blk = lambda w: pl.BlockSpec((1, tq, w), lambda b, i: (b, i, 0))
    full = lambda w: pl.BlockSpec((1, L, w), lambda b, i: (b, 0, 0))
    return pl.pallas_call(
        functools.partial(_dsa_attn_kernel, n_sel=n_sel, tq=tq, W=W),
        grid=(B, nb),
        in_specs=[blk(BRANCH), blk(IDX_HEADS * IDX_DIM), blk(IDX_HEADS), full(DSA_KV), full(DSA_KV), full(IDX_DIM)],
        out_specs=blk(BRANCH),
        out_shape=jax.ShapeDtypeStruct((B, L, BRANCH), BF16),
        scratch_shapes=[pltpu.VMEM((L // W, tq, W), F32), pltpu.VMEM((IDX_HEADS, tq, LANES), F32)],
        compiler_params=_params("parallel", "arbitrary"),
    )(q, qi, wi, k, v, ki)


def _dsa_dec_kernel(pt_ref, q_ref, qi_ref, wi_ref, kn_ref, vn_ref, kin_ref, *rest, pps, n_pages, dl, n_sel):
    k_refs, v_refs, ki_refs = rest[0:pps], rest[pps:2 * pps], rest[2 * pps:3 * pps]
    o_ref = rest[3 * pps]
    ks_ref, vs_ref, kis_ref = rest[3 * pps + 1:]
    p = pl.program_id(1)
    past = n_pages * PAGE
    nk = past + PAGE
    G = DSA_KV_HEADS

    for r in range(pps):
        page = p * pps + r
        start = pl.multiple_of(page * (PAGE * G), PAGE * G)
        ks_ref[pl.ds(start, PAGE * G), :] = k_refs[r][...]
        vs_ref[pl.ds(start, PAGE * G), :] = v_refs[r][...]
        kis_ref[page] = ki_refs[r][...]

    @pl.when(p == n_pages // pps - 1)
    def _():
        for ref, new in ((ks_ref, kn_ref), (vs_ref, vn_ref)):
            ref[past * G:(past + dl) * G, :] = new[...]
            ref[(past + dl) * G:nk * G, :] = jnp.zeros(((PAGE - dl) * G, HEAD_DIM), F32)
        kis_ref[n_pages] = kin_ref[0]

        qi = qi_ref[0]
        wi = wi_ref[0]
        chunks = []
        for c in range(n_pages + 1):
            wr = wi * jnp.maximum(_dot(qi, kis_ref[c].astype(BF16)), 0.0)
            sc = wr[0:dl]
            for h in range(1, IDX_HEADS):
                sc = sc + wr[h * dl:(h + 1) * dl]
            chunks.append(sc)
        score = jnp.concatenate(chunks, axis=1)
        valid = _iota((dl, nk), 1) <= past + _iota((dl, nk), 0)
        score = jnp.where(valid, score, -jnp.inf)

        def count(mask):
            return jnp.sum(jnp.where(mask, 1.0, 0.0), axis=-1, keepdims=True)

        thr = _kth_largest(lambda f: count(score >= f), (dl, 1), n_sel, 2)

        def drop_surplus_ties(sc):
            need = n_sel - count(sc > thr)
            upper = _strict_upper(PAGE)
            seen, kept = jnp.zeros((dl, 1), F32), []
            for c in range(n_pages + 1):
                scc = sc[:, c * PAGE:(c + 1) * PAGE]
                eq = scc == thr
                eqf = jnp.where(eq, 1.0, 0.0)
                rank = seen + _dot(eqf.astype(BF16), upper)
                kept.append(jnp.where(eq & (rank >= need), -jnp.inf, scc))
                seen = seen + jnp.sum(eqf, axis=-1, keepdims=True)
            return jnp.concatenate(kept, axis=1)

        score = lax.cond(jnp.max(count(score >= thr)) > n_sel, drop_surplus_ties, lambda sc: sc, score)
        sel = score >= thr

        for g in range(G):
            qg = jnp.concatenate(
                [q_ref[0, :, (g * DSA_GROUP + r) * HEAD_DIM:(g * DSA_GROUP + r + 1) * HEAD_DIM].astype(F32)
                 for r in range(DSA_GROUP)], axis=0).astype(BF16)
            s = _dot_nt(qg, ks_ref[_head_rows(g, nk, G), :].astype(BF16)).reshape(DSA_GROUP, dl, nk)
            s = jnp.where(sel[None], s, NEG)
            pr = jnp.exp(s - jnp.max(s, axis=-1, keepdims=True))
            den = jnp.sum(pr, axis=-1, keepdims=True)
            pv = _dot(pr.reshape(DSA_GROUP * dl, nk).astype(BF16), vs_ref[_head_rows(g, nk, G), :].astype(BF16))
            out = pv.reshape(DSA_GROUP, dl, HEAD_DIM) / den
            for r in range(DSA_GROUP):
                hh = g * DSA_GROUP + r
                o_ref[0, :, hh * HEAD_DIM:(hh + 1) * HEAD_DIM] = out[r].astype(BF16)


def _dsa_attn_decode(q, qi, wi, kn, vn, kin, cache_k, cache_v, cache_ki, layer, pt_flat, n_pages, pps, n_sel):
    DB, DL, _ = q.shape
    nk = n_pages * PAGE + PAGE
    G = DSA_KV_HEADS
    qi_hq = jnp.swapaxes(qi.reshape(DB, DL, IDX_HEADS, IDX_DIM), 1, 2).reshape(DB, IDX_HEADS * DL, IDX_DIM)
    wi_hq = jnp.swapaxes(wi.reshape(DB, DL, IDX_HEADS), 1, 2).reshape(DB, IDX_HEADS * DL, 1)
    kin_t = jnp.pad(jnp.swapaxes(kin, 1, 2), ((0, 0), (0, 0), (0, PAGE - DL)))

    def page_spec(rows, r):
        return pl.BlockSpec((None, None, rows, LANES),
                            lambda b, p, pt, r=r: (layer, pt[b * n_pages + p * pps + r], 0, 0))

    tok = lambda n, w: pl.BlockSpec((1, n, w), lambda b, p, pt: (b, 0, 0))
    new_rows = pl.BlockSpec((DL * G, HEAD_DIM), lambda b, p, pt: (b, 0))
    in_specs = [tok(DL, BRANCH), tok(IDX_HEADS * DL, IDX_DIM), tok(IDX_HEADS * DL, 1),
                new_rows, new_rows, tok(IDX_DIM, PAGE)]
    in_specs += [page_spec(PAGE * G, r) for r in range(pps)] * 2 + [page_spec(IDX_DIM, r) for r in range(pps)]
    grid_spec = pltpu.PrefetchScalarGridSpec(
        num_scalar_prefetch=1, grid=(DB, n_pages // pps), in_specs=in_specs, out_specs=tok(DL, BRANCH),
        scratch_shapes=[pltpu.VMEM((nk * G, HEAD_DIM), F32), pltpu.VMEM((nk * G, HEAD_DIM), F32),
                        pltpu.VMEM((n_pages + 1, IDX_DIM, PAGE), F32)])
    return pl.pallas_call(
        functools.partial(_dsa_dec_kernel, pps=pps, n_pages=n_pages, dl=DL, n_sel=n_sel),
        grid_spec=grid_spec,
        out_shape=jax.ShapeDtypeStruct((DB, DL, BRANCH), BF16),
        compiler_params=_params("parallel", "arbitrary"),
    )(pt_flat, q, qi_hq, wi_hq, kn, vn, kin_t, *([cache_k] * pps), *([cache_v] * pps), *([cache_ki] * pps))


def _gdn_kernel(x_ref, z_ref, g_ref, b_ref, cw_ref, nw_ref, cs_ref, s0_ref, o_ref, sout_ref, xb_ref, s_ref,
                *, C, nc, bb):
    c = pl.program_id(1)
    HIST = SUBLANES

    @pl.when(c == 0)
    def _():
        xb_ref[:, 0:HIST, :] = cs_ref[...]
        s_ref[...] = s0_ref[...]

    incl, strict = _tril(C), _tril(C, strict=True)
    eye_c = (_iota((C, C), 0) == _iota((C, C), 1)).astype(F32)
    eye_h = (_iota((N_HEADS, N_HEADS), 0) == _iota((N_HEADS, N_HEADS), 1)).astype(F32)
    n_dbl = int(math.log2(C)) - 1

    convs, G_alls, Gt_alls = [], [], []
    for e in range(bb):
        xb_ref[e, HIST:HIST + C, :] = x_ref[e]
        conv = xb_ref[e, HIST:HIST + C, :] * cw_ref[CONV_W - 1:CONV_W, :]
        for j in range(CONV_W - 1):
            off = HIST - (CONV_W - 1) + j
            conv = conv + xb_ref[e, off:off + C, :] * cw_ref[j:j + 1, :]
        hist = xb_ref[e, C:C + HIST, :]
        xb_ref[e, 0:HIST, :] = hist
        convs.append(_silu(conv))
        G = _dot(incl.astype(F32), g_ref[e], HIGHEST)
        G_alls.append(G)
        Gt_alls.append(_dot_nt(eye_h, G, HIGHEST))

    units = [(e, h) for e in range(bb) for h in range(N_HEADS)]

    def stage(fn):
        return [fn(i, e, h) for i, (e, h) in enumerate(units)]

    def head(e, j, h):
        return convs[e][:, j * BRANCH + h * HEAD_DIM:j * BRANCH + (h + 1) * HEAD_DIM]

    def l2n(a):
        return a * lax.rsqrt(jnp.sum(a * a, axis=-1, keepdims=True) + EPS)

    q = stage(lambda i, e, h: l2n(head(e, 0, h)) * ATTN_SCALE)
    k = stage(lambda i, e, h: l2n(head(e, 1, h)))
    v = stage(lambda i, e, h: head(e, 2, h))
    Gc = stage(lambda i, e, h: G_alls[e][:, h:h + 1])
    bc = stage(lambda i, e, h: b_ref[e][:, h:h + 1])
    decay = stage(lambda i, e, h: jnp.exp(jnp.where(incl, Gc[i] - Gt_alls[e][h:h + 1, :], -jnp.inf)))
    kk = stage(lambda i, e, h: _dot1(k[i], k[i], _dot_nt))
    A = stage(lambda i, e, h: jnp.where(strict, bc[i] * kk[i] * decay[i], 0.0))
    T = stage(lambda i, e, h: eye_c - A[i])
    P = stage(lambda i, e, h: _dot1(A[i], A[i]))
    for it in range(n_dbl):
        TP = stage(lambda i, e, h: _dot1(T[i], P[i]))
        T = stage(lambda i, e, h: T[i] + TP[i])
        if it + 1 < n_dbl:
            P = stage(lambda i, e, h: _dot1(P[i], P[i]))
    eG = stage(lambda i, e, h: jnp.exp(Gc[i]))
    WU = stage(lambda i, e, h: _dot1(T[i], jnp.concatenate([k[i] * (bc[i] * eG[i]), v[i] * bc[i]], axis=1)))
    Aqk = stage(lambda i, e, h: _dot1(q[i], k[i], _dot_nt) * decay[i])
    G_last = stage(lambda i, e, h: Gc[i][C - 1:C, :])
    k_dec = stage(lambda i, e, h: k[i] * jnp.exp(G_last[i] - Gc[i]))
    S = stage(lambda i, e, h: s_ref[e, h])
    WqS = stage(lambda i, e, h: _dot1(jnp.concatenate([WU[i][:, 0:HEAD_DIM], q[i]], axis=0), S[i]))
    Vn = stage(lambda i, e, h: WU[i][:, HEAD_DIM:2 * HEAD_DIM] - WqS[i][0:C])
    o = stage(lambda i, e, h: eG[i] * WqS[i][C:2 * C] + _dot1(Aqk[i], Vn[i]))
    S_new = stage(lambda i, e, h: jnp.exp(G_last[i]) * S[i] + _dot1(k_dec[i], Vn[i], _dot_tn))
    for i, (e, h) in enumerate(units):
        sl = slice(h * HEAD_DIM, (h + 1) * HEAD_DIM)
        s_ref[e, h] = S_new[i]
        on = o[i] * lax.rsqrt(jnp.mean(o[i] * o[i], axis=-1, keepdims=True) + EPS) * nw_ref[...]
        o_ref[e, :, sl] = (on * _silu(z_ref[e, :, sl].astype(F32))).astype(BF16)

    @pl.when(c == nc - 1)
    def _():
        sout_ref[...] = s_ref[...]


def _gdn_mix(qkv, z, g, beta, conv_w, norm_w, conv_state, S0):
    B, L, _ = qkv.shape
    C = min(GDN_CHUNK, L)
    nc = L // C
    assert C & (C - 1) == 0 and L % C == 0
    bb = max(d for d in (1, 2, 4) if B % d == 0 and d * C <= GDN_CHUNK)
    cs = jnp.pad(conv_state, ((0, 0), (SUBLANES - (CONV_W - 1), 0), (0, 0)))
    blk = lambda w: pl.BlockSpec((bb, C, w), lambda b, c: (b, c, 0))
    state = pl.BlockSpec((bb, N_HEADS, HEAD_DIM, HEAD_DIM), lambda b, c: (b, 0, 0, 0))
    return pl.pallas_call(
        functools.partial(_gdn_kernel, C=C, nc=nc, bb=bb),
        grid=(B // bb, nc),
        in_specs=[blk(3 * BRANCH), blk(BRANCH), blk(N_HEADS), blk(N_HEADS),
                  pl.BlockSpec((CONV_W, 3 * BRANCH), lambda b, c: (0, 0)),
                  pl.BlockSpec((1, HEAD_DIM), lambda b, c: (0, 0)),
                  pl.BlockSpec((bb, SUBLANES, 3 * BRANCH), lambda b, c: (b, 0, 0)),
                  state],
        out_specs=[blk(BRANCH), state],
        out_shape=[jax.ShapeDtypeStruct((B, L, BRANCH), BF16),
                   jax.ShapeDtypeStruct((B, N_HEADS, HEAD_DIM, HEAD_DIM), F32)],
        scratch_shapes=[pltpu.VMEM((bb, C + SUBLANES, 3 * BRANCH), F32),
                        pltpu.VMEM((bb, N_HEADS, HEAD_DIM, HEAD_DIM), F32)],
        compiler_params=_params("parallel", "arbitrary"),
    )(qkv, z, g, beta, conv_w, norm_w.reshape(1, HEAD_DIM), cs, S0)


def _pad_cols(w, n=LANES):
    return jnp.pad(w, ((0, 0), (0, n - w.shape[1])))


def _row_tile(T):
    return 256 if T % 256 == 0 else T


def kernel(x_prompt, x_sample, cache_fox_k, cache_fox_v, cache_fox_logf, cache_dsa_k, cache_dsa_v, cache_dsa_kidx, state_gdn_conv, state_gdn_S, page_table, norm_w, norm_f, fox_w_in, fox_b_f, fox_w_out, dsa_w_in, dsa_w_out, gdn_w_in, gdn_conv_w, gdn_A_log, gdn_dt_bias, gdn_norm_w, gdn_w_out):
    B, L, _ = x_prompt.shape
    DB, DL, _ = x_sample.shape
    n_pages = page_table.shape[1]
    past = n_pages * PAGE
    depth = norm_w.shape[0]
    pt_flat = page_table.reshape(-1)
    pages_per_step = lambda want: max(p for p in (1, 2, 4, 8) if p <= want and n_pages % p == 0)
    fox_pps, dsa_pps = pages_per_step(8), pages_per_step(4)
    groups = ((B, L), (DB, DL))
    xs = [x_prompt.reshape(B * L, D_MODEL), x_sample.reshape(DB * DL, D_MODEL)]
    tms = [_row_tile(nb * nl) for nb, nl in groups]
    pos = [jnp.arange(L), past + jnp.arange(DL)]
    tq_p = max(t for t in (PAGE, 2 * PAGE, 4 * PAGE) if L % t == 0)

    pool = cache_fox_k.shape[1]
    fox_ck = cache_fox_k.reshape(-1, pool, PAGE * N_HEADS, HEAD_DIM)
    fox_cv = cache_fox_v.reshape(-1, pool, PAGE * N_HEADS, HEAD_DIM)
    fox_clf = jnp.swapaxes(cache_fox_logf, 2, 3)
    dsa_ck = cache_dsa_k.reshape(-1, pool, PAGE * DSA_KV_HEADS, HEAD_DIM)
    dsa_cv = cache_dsa_v.reshape(-1, pool, PAGE * DSA_KV_HEADS, HEAD_DIM)
    dsa_cki = jnp.swapaxes(cache_dsa_kidx, 2, 3)

    st = {name: ([], []) for name in ("fox_k", "fox_v", "fox_lf", "dsa_k", "dsa_v", "dsa_ki", "gdn_conv", "gdn_S")}
    finals = [None, None]

    for i in range(depth):
        kind, j = i % 3, i // 3
        nw = norm_w[i].reshape(1, D_MODEL)
        last = i == depth - 1
        for gi, (nb, nl) in enumerate(groups):
            x2d, tm = xs[gi], tms[gi]
            T = nb * nl
            prompt = gi == 0
            sh = lambda a: a.reshape(nb, nl, a.shape[-1])
            if kind == 0:
                w = fox_w_in[j]
                w_main = jnp.concatenate([w[:, :3 * BRANCH], w[:, 3 * BRANCH + N_HEADS:]], axis=1).astype(BF16)
                w_small = _pad_cols(w[:, 3 * BRANCH:3 * BRANCH + N_HEADS]).astype(BF16)
                outs = [(1, BRANCH, BF16), (N_HEADS, HEAD_DIM, F32), (N_HEADS, HEAD_DIM, F32), (1, BRANCH, BF16),
                        (1, N_HEADS, F32)]
                if prompt:
                    outs += [(1, BRANCH, BF16), (1, BRANCH, BF16)]
                res = _in_proj(_fox_in_kernel, x2d, [nw, w_main, w_small, fox_b_f[j].reshape(1, N_HEADS)], [], outs, tm)
                q, k, v, g, lf = res[:5]
                if prompt:
                    o = _fox_attn_prompt(sh(q), sh(res[5]), sh(res[6]), _cumsum_seq(sh(lf)), tq_p)
                else:
                    o = _fox_attn_decode(sh(q), k, v, sh(lf), fox_ck, fox_cv, fox_clf, j, pt_flat, n_pages, fox_pps)
                st["fox_k"][gi].append(k.reshape(nb, nl, N_HEADS, HEAD_DIM))
                st["fox_v"][gi].append(v.reshape(nb, nl, N_HEADS, HEAD_DIM))
                st["fox_lf"][gi].append(lf.reshape(nb, nl, N_HEADS))
                w_out = fox_w_out[j]
            elif kind == 1:
                w = dsa_w_in[j]
                o_qi = BRANCH + 2 * DSA_KV
                o_wi = o_qi + IDX_HEADS * IDX_DIM
                o_ki = o_wi + IDX_HEADS
                o_g = o_ki + IDX_DIM
                w_main = jnp.concatenate([w[:, :o_wi], w[:, o_g:]], axis=1).astype(BF16)
                w_small = _pad_cols(jnp.concatenate([w[:, o_ki:o_g], w[:, o_wi:o_ki]], axis=1)).astype(BF16)
                tables = list(_rope_tables(pos[gi], HEAD_DIM, tm) + _rope_tables(pos[gi], IDX_DIM, tm))
                outs = [(1, BRANCH, BF16), (DSA_KV_HEADS, HEAD_DIM, F32), (DSA_KV_HEADS, HEAD_DIM, F32),
                        (1, IDX_HEADS * IDX_DIM, BF16), (1, BRANCH, BF16), (1, IDX_DIM, F32), (1, IDX_HEADS, F32)]
                if prompt:
                    outs += [(1, DSA_KV, BF16), (1, DSA_KV, BF16)]
                res = _in_proj(_dsa_in_kernel, x2d, [nw, w_main, w_small], tables, outs, tm)
                q, k, v, qi, g, ki, wi = res[:7]
                if prompt:
                    o = _dsa_attn_prompt(sh(q), sh(qi), sh(wi), sh(res[7]), sh(res[8]), sh(ki),
                                         min(TOPK_MAX, L // 4), PAGE)
                else:
                    o = _dsa_attn_decode(sh(q), sh(qi), sh(wi), k, v, sh(ki), dsa_ck, dsa_cv, dsa_cki,
                                         j, pt_flat, n_pages, dsa_pps, min(TOPK_MAX, (past + DL) // 4))
                st["dsa_k"][gi].append(k.reshape(nb, nl, DSA_KV_HEADS, HEAD_DIM))
                st["dsa_v"][gi].append(v.reshape(nb, nl, DSA_KV_HEADS, HEAD_DIM))
                st["dsa_ki"][gi].append(ki.reshape(nb, nl, IDX_DIM))
                w_out = dsa_w_out[j]
            else:
                w = gdn_w_in[j]
                w_main = jnp.concatenate([w[:, :3 * BRANCH], w[:, 3 * BRANCH + 2 * N_HEADS:]], axis=1).astype(BF16)
                w_small = _pad_cols(w[:, 3 * BRANCH:3 * BRANCH + 2 * N_HEADS]).astype(BF16)
                qkv, z, gg, beta = _in_proj(
                    _gdn_in_kernel, x2d,
                    [nw, w_main, w_small, gdn_A_log[j].reshape(1, N_HEADS), gdn_dt_bias[j].reshape(1, N_HEADS)], [],
                    [(1, 3 * BRANCH, F32), (1, BRANCH, BF16), (1, N_HEADS, F32), (1, N_HEADS, F32)], tm)
                if prompt:
                    conv0 = jnp.zeros((nb, CONV_W - 1, 3 * BRANCH), F32)
                    S0 = jnp.zeros((nb, N_HEADS, HEAD_DIM, HEAD_DIM), F32)
                else:
                    conv0, S0 = state_gdn_conv[j], state_gdn_S[j]
                o, S_new = _gdn_mix(sh(qkv), sh(z), sh(gg), sh(beta), gdn_conv_w[j], gdn_norm_w[j], conv0, S0)
                tail = jnp.concatenate([conv0, sh(qkv)[:, -min(nl, CONV_W - 1):]], axis=1)
                st["gdn_conv"][gi].append(tail[:, -(CONV_W - 1):])
                st["gdn_S"][gi].append(S_new)
                g = None
                w_out = gdn_w_out[j]
            y = _out_proj(o.reshape(T, BRANCH), g, x2d, w_out.astype(BF16), norm_f if last else None, tm)
            if last:
                finals[gi] = y.reshape(nb, nl, D_MODEL)
            else:
                xs[gi] = y

    stk = lambda name, gi: jnp.stack(st[name][gi], axis=0)
    return (finals[0], finals[1],
            stk("fox_k", 0), stk("fox_v", 0), stk("fox_lf", 0),
            stk("fox_k", 1), stk("fox_v", 1), stk("fox_lf", 1),
            stk("dsa_k", 0), stk("dsa_v", 0), stk("dsa_ki", 0),
            stk("dsa_k", 1), stk("dsa_v", 1), stk("dsa_ki", 1),
            stk("gdn_conv", 0), stk("gdn_S", 0),
            stk("gdn_conv", 1), stk("gdn_S", 1))
```

```python
import functools
import math

import jax
import jax.numpy as jnp
from jax import lax
from jax.experimental import pallas as pl
from jax.experimental.pallas import tpu as pltpu

F32 = jnp.float32
BF16 = jnp.bfloat16
HIGHEST = lax.Precision.HIGHEST

D_MODEL = 1024
N_HEADS = 8
HEAD_DIM = 128
BRANCH = N_HEADS * HEAD_DIM
ATTN_SCALE = HEAD_DIM ** -0.5
DSA_KV_HEADS = 2
DSA_GROUP = N_HEADS // DSA_KV_HEADS
DSA_KV = DSA_KV_HEADS * HEAD_DIM
IDX_HEADS = 8
IDX_DIM = 64
IDX_SCALE = IDX_DIM ** -0.5
TOPK_MAX = 256
CONV_W = 4
GDN_CHUNK = 64
ROPE_THETA = 10000.0
EPS = 1e-6
PAGE = 128
LANES = 128
SUBLANES = 8
NEG = -1e30
INT_MIN = -(2 ** 31)
NEG_INF_KEY = INT_MIN + 0x007FFFFF
V7X_VMEM_BYTES = 64 * 1024 * 1024
VMEM_LIMIT = V7X_VMEM_BYTES * 7 // 8


def _dot(a, b, prec=None):
    return jnp.dot(a, b, preferred_element_type=F32, precision=prec)


def _dot_nt(a, b, prec=None):
    return lax.dot_general(a, b, (((1,), (1,)), ((), ())), preferred_element_type=F32, precision=prec)


def _dot_tn(a, b, prec=None):
    return lax.dot_general(a, b, (((0,), (0,)), ((), ())), preferred_element_type=F32, precision=prec)


def _dot1(a, b, dot=_dot):
    return dot(a.astype(BF16), b.astype(BF16))


def _iota(shape, axis):
    return lax.broadcasted_iota(jnp.int32, shape, axis)


def _tril(n, strict=False):
    r, c = _iota((n, n), 0), _iota((n, n), 1)
    return (r > c) if strict else (r >= c)


def _sigmoid(x):
    return 1.0 / (1.0 + jnp.exp(-x))


def _silu(x):
    return x * _sigmoid(x)


def _softplus(x):
    return jnp.maximum(x, 0.0) + jnp.log1p(jnp.exp(-jnp.abs(x)))


def _params(*sem):
    return pltpu.CompilerParams(dimension_semantics=sem, vmem_limit_bytes=VMEM_LIMIT)


def _head_rows(h, n_tok, n_heads):
    return pl.ds(h, n_tok, stride=n_heads)


def _normed(x_ref, nw_ref):
    x = x_ref[...]
    ms = jnp.mean(x * x, axis=-1, keepdims=True)
    return (x * lax.rsqrt(ms + EPS) * nw_ref[...]).astype(BF16)


def _fox_in_kernel(x_ref, nw_ref, w_ref, ws_ref, bf_ref, q_ref, k_ref, v_ref, g_ref, lf_ref, *copies):
    tm = x_ref.shape[0]
    h = _normed(x_ref, nw_ref)
    q_ref[...] = (_dot(h, w_ref[:, 0:BRANCH]) * ATTN_SCALE).astype(BF16)
    for j, ref in ((1, k_ref), (2, v_ref)):
        for hh in range(N_HEADS):
            col = j * BRANCH + hh * HEAD_DIM
            r = _dot(h, w_ref[:, col:col + HEAD_DIM])
            ref[_head_rows(hh, tm, N_HEADS), :] = r
            if copies:
                copies[j - 1][:, hh * HEAD_DIM:(hh + 1) * HEAD_DIM] = r.astype(BF16)
    g_ref[...] = _dot(h, w_ref[:, 3 * BRANCH:4 * BRANCH]).astype(BF16)
    f = _dot(h, ws_ref[...])[:, 0:N_HEADS] + bf_ref[...]
    lf_ref[...] = -_softplus(-f)


def _rope128(x, cos, sin_signed):
    return x * cos + pltpu.roll(x, HEAD_DIM // 2, 1) * sin_signed


def _rope64(x, cos, sin_signed):
    lane = _iota(x.shape, 1)
    rot = jnp.where((lane & (IDX_DIM - 1)) < IDX_DIM // 2,
                    pltpu.roll(x, LANES - IDX_DIM // 2, 1), pltpu.roll(x, IDX_DIM // 2, 1))
    return x * cos + rot * sin_signed


def _dsa_in_kernel(x_ref, nw_ref, w_ref, ws_ref, c128_ref, s128_ref, c64_ref, s64_ref,
                   q_ref, k_ref, v_ref, qi_ref, g_ref, ki_ref, wi_ref, *copies):
    tm = x_ref.shape[0]
    h = _normed(x_ref, nw_ref)
    c128, s128, c64, s64 = c128_ref[...], s128_ref[...], c64_ref[...], s64_ref[...]
    o_k, o_v, o_qi, o_g = BRANCH, BRANCH + DSA_KV, BRANCH + 2 * DSA_KV, BRANCH + 2 * DSA_KV + IDX_HEADS * IDX_DIM
    for hh in range(N_HEADS):
        sl = slice(hh * HEAD_DIM, (hh + 1) * HEAD_DIM)
        q_ref[:, sl] = (_rope128(_dot(h, w_ref[:, sl]), c128, s128) * ATTN_SCALE).astype(BF16)
    for hh in range(DSA_KV_HEADS):
        sl = slice(hh * HEAD_DIM, (hh + 1) * HEAD_DIM)
        kk = _rope128(_dot(h, w_ref[:, o_k + hh * HEAD_DIM:o_k + (hh + 1) * HEAD_DIM]), c128, s128)
        vv = _dot(h, w_ref[:, o_v + hh * HEAD_DIM:o_v + (hh + 1) * HEAD_DIM])
        k_ref[_head_rows(hh, tm, DSA_KV_HEADS), :] = kk
        v_ref[_head_rows(hh, tm, DSA_KV_HEADS), :] = vv
        if copies:
            copies[0][:, sl] = kk.astype(BF16)
            copies[1][:, sl] = vv.astype(BF16)
    for hh in range(IDX_HEADS * IDX_DIM // LANES):
        sl = slice(hh * LANES, (hh + 1) * LANES)
        qi_ref[:, sl] = _rope64(_dot(h, w_ref[:, o_qi + hh * LANES:o_qi + (hh + 1) * LANES]), c64, s64).astype(BF16)
    g_ref[...] = _dot(h, w_ref[:, o_g:o_g + BRANCH]).astype(BF16)
    small = _dot(h, ws_ref[...])
    ki_ref[...] = _rope64(small, c64, s64)[:, 0:IDX_DIM]
    wi_ref[...] = small[:, IDX_DIM:IDX_DIM + IDX_HEADS] * (IDX_HEADS ** -0.5 * IDX_SCALE)


def _gdn_in_kernel(x_ref, nw_ref, w_ref, ws_ref, alog_ref, dtb_ref, qkv_ref, z_ref, g_ref, beta_ref):
    h = _normed(x_ref, nw_ref)
    for j in range(3):
        sl = slice(j * BRANCH, (j + 1) * BRANCH)
        qkv_ref[:, sl] = _dot(h, w_ref[:, sl])
    z_ref[...] = _dot(h, w_ref[:, 3 * BRANCH:4 * BRANCH]).astype(BF16)
    small = _dot(h, ws_ref[...])
    a = small[:, 0:N_HEADS]
    b = small[:, N_HEADS:2 * N_HEADS]
    g_ref[...] = -jnp.exp(alog_ref[...]) * _softplus(a + dtb_ref[...])
    beta_ref[...] = _sigmoid(b)


def _row_spec(tm, n):
    return pl.BlockSpec((tm, n), lambda i: (i, 0))


def _full_spec(shape):
    return pl.BlockSpec(shape, lambda i: (0,) * len(shape))


def _in_proj(kern, x2d, consts, tables, outs, tm):
    T = x2d.shape[0]
    in_specs = [_row_spec(tm, D_MODEL)] + [_full_spec(c.shape) for c in consts[:3]]
    args = [x2d] + list(consts[:3])
    for t in tables:
        nblk = t.shape[0] // tm
        in_specs.append(pl.BlockSpec((tm, LANES), lambda i, nblk=nblk: (i % nblk, 0)))
        args.append(t)
    for c in consts[3:]:
        in_specs.append(_full_spec(c.shape))
        args.append(c)
    return pl.pallas_call(
        kern,
        grid=(T // tm,),
        in_specs=in_specs,
        out_specs=[_row_spec(tm * r, n) for r, n, _ in outs],
        out_shape=[jax.ShapeDtypeStruct((T * r, n), dt) for r, n, dt in outs],
        compiler_params=_params("parallel"),
    )(*args)


def _rope_tables(pos, head_dim, tm):
    half = head_dim // 2
    inv = ROPE_THETA ** (-jnp.arange(half, dtype=F32) / half)
    ang = pos.astype(F32)[:, None] * inv[None, :]
    cos, sin = jnp.cos(ang), jnp.sin(ang)
    reps = LANES // head_dim
    cf = jnp.tile(jnp.concatenate([cos, cos], axis=-1), (1, reps))
    ss = jnp.tile(jnp.concatenate([-sin, sin], axis=-1), (1, reps))
    if cf.shape[0] < tm:
        cf = jnp.tile(cf, (tm // cf.shape[0], 1))
        ss = jnp.tile(ss, (tm // ss.shape[0], 1))
    return cf, ss


def _out_kernel(*refs, gated, final):
    refs = list(refs)
    o_ref = refs.pop(0)
    g_ref = refs.pop(0) if gated else None
    x_ref, w_ref = refs.pop(0), refs.pop(0)
    nf_ref = refs.pop(0) if final else None
    y_ref = refs.pop(0)
    o = o_ref[...]
    if gated:
        o = (o.astype(F32) * _silu(g_ref[...].astype(F32))).astype(BF16)
    y = x_ref[...] + _dot(o, w_ref[...])
    if final:
        ms = jnp.mean(y * y, axis=-1, keepdims=True)
        y = y * lax.rsqrt(ms + EPS) * nf_ref[...]
    y_ref[...] = y


def _out_proj(o, g, x2d, w_bf16, norm_f, tm):
    T = x2d.shape[0]
    gated, final = g is not None, norm_f is not None
    args, specs = [o], [_row_spec(tm, BRANCH)]
    if gated:
        args.append(g)
        specs.append(_row_spec(tm, BRANCH))
    args += [x2d, w_bf16]
    specs += [_row_spec(tm, D_MODEL), _full_spec(w_bf16.shape)]
    if final:
        args.append(norm_f.reshape(1, D_MODEL))
        specs.append(_full_spec((1, D_MODEL)))
    return pl.pallas_call(
        functools.partial(_out_kernel, gated=gated, final=final),
        grid=(T // tm,), in_specs=specs, out_specs=_row_spec(tm, D_MODEL),
        out_shape=jax.ShapeDtypeStruct((T, D_MODEL), F32),
        compiler_params=_params("parallel"),
    )(*args)


def _cumsum_kernel(x_ref, o_ref, *, n_blk):
    tril = _tril(PAGE).astype(F32)
    carry = jnp.zeros((1, N_HEADS), F32)
    for c in range(n_blk):
        y = _dot(tril, x_ref[0, c * PAGE:(c + 1) * PAGE, :], HIGHEST) + carry
        o_ref[0, c * PAGE:(c + 1) * PAGE, :] = y
        carry = y[PAGE - 1:PAGE, :]


def _cumsum_seq(logf):
    B, L, H = logf.shape
    return pl.pallas_call(
        functools.partial(_cumsum_kernel, n_blk=L // PAGE),
        grid=(B,),
        in_specs=[pl.BlockSpec((1, L, H), lambda b: (b, 0, 0))],
        out_specs=pl.BlockSpec((1, L, H), lambda b: (b, 0, 0)),
        out_shape=jax.ShapeDtypeStruct((B, L, H), F32),
        compiler_params=_params("parallel"),
    )(logf)


def _softmax_step(s, m, l, acc, pv_fn):
    m_new = jnp.maximum(m, jnp.max(s, axis=-1, keepdims=True))
    alpha = jnp.exp(m - m_new)
    p = jnp.exp(s - m_new)
    l = alpha * l + jnp.sum(p, axis=-1, keepdims=True)
    acc = alpha * acc + pv_fn(p.astype(BF16))
    return m_new, l, acc


def _fox_attn_kernel(q_ref, k_ref, v_ref, ck_ref, o_ref, *, tq, tk, ts):
    i = pl.program_id(2)
    n_sub = tq // ts
    qs = [q_ref[0, r * ts:(r + 1) * ts, :] for r in range(n_sub)]

    def step(j, carry, masked):
        start = pl.multiple_of(j * tk, tk)
        k = k_ref[0, pl.ds(start, tk), :]
        v = v_ref[0, pl.ds(start, tk), :]
        ck = ck_ref[0, 0, j]
        out = []
        for r in range(n_sub):
            s = _dot_nt(qs[r], k) - ck
            if masked:
                s = jnp.where(start + _iota((ts, tk), 1) <= i * tq + r * ts + _iota((ts, tk), 0), s, NEG)
            out.append(_softmax_step(s, *carry[r], lambda p: _dot(p, v)))
        return tuple(out)

    carry = tuple((jnp.full((ts, 1), NEG, F32), jnp.zeros((ts, 1), F32), jnp.zeros((ts, HEAD_DIM), F32))
                  for _ in range(n_sub))
    n_diag = tq // tk
    carry = lax.fori_loop(0, i * n_diag, lambda j, c: step(j, c, False), carry)
    for d in range(n_diag):
        carry = step(i * n_diag + d, carry, True)
    for r, (_, l, acc) in enumerate(carry):
        o_ref[0, r * ts:(r + 1) * ts, :] = (acc / l).astype(BF16)


def _fox_attn_prompt(q, k, v, cum, tq, tk, ts):
    B, L, _ = q.shape
    nb = L // tq
    ck = jnp.swapaxes(cum, 1, 2).reshape(B, N_HEADS, L // tk, 1, tk)
    return pl.pallas_call(
        functools.partial(_fox_attn_kernel, tq=tq, tk=tk, ts=ts),
        grid=(B, N_HEADS, nb),
        in_specs=[
            pl.BlockSpec((1, tq, HEAD_DIM), lambda b, h, i: (b, i, h)),
            pl.BlockSpec((1, L, HEAD_DIM), lambda b, h, i: (b, 0, h)),
            pl.BlockSpec((1, L, HEAD_DIM), lambda b, h, i: (b, 0, h)),
            pl.BlockSpec((1, 1, L // tk, 1, tk), lambda b, h, i: (b, h, 0, 0, 0)),
        ],
        out_specs=pl.BlockSpec((1, tq, HEAD_DIM), lambda b, h, i: (b, i, h)),
        out_shape=jax.ShapeDtypeStruct((B, L, BRANCH), BF16),
        compiler_params=_params("parallel", "parallel", "arbitrary"),
    )(q, k, v, ck)


def _cumsum_lanes(x):
    lane = _iota(x.shape, 1)
    d = 1
    while d < x.shape[1]:
        x = x + jnp.where(lane >= d, pltpu.roll(x, d, 1), 0.0)
        d *= 2
    return x


def _fox_dec_kernel(pt_ref, q_ref, kn_ref, vn_ref, lfn_ref, *rest, pps, n_steps, dl):
    k_refs, v_refs, lf_refs = rest[0:pps], rest[pps:2 * pps], rest[2 * pps:3 * pps]
    o_ref = rest[3 * pps]
    m_ref, l_ref, acc_ref, car_ref = rest[3 * pps + 1:]
    p = pl.program_id(1)
    nr = N_HEADS * dl
    n_chain = m_ref.shape[0]

    @pl.when(p == 0)
    def _():
        m_ref[...] = jnp.full(m_ref.shape, NEG, F32)
        l_ref[...] = jnp.zeros(l_ref.shape, F32)
        acc_ref[...] = jnp.zeros(acc_ref.shape, F32)
        car_ref[...] = jnp.zeros((N_HEADS, LANES), F32)

    qs = [q_ref[0, :, h * HEAD_DIM:(h + 1) * HEAD_DIM] for h in range(N_HEADS)]

    def attend(chains, n_tok, mask):
        off, cums = car_ref[...], {}
        for ci, (_, blocks) in enumerate(chains):
            for bi, (_, _, lf) in enumerate(blocks):
                c = _cumsum_lanes(lf)
                cums[ci, bi] = (c + off)[:, 0:n_tok]
                off = off + jnp.broadcast_to(c[:, LANES - 1:LANES], (N_HEADS, LANES))
        car_ref[...] = off

        def head_rows(blocks, which, h):
            return jnp.concatenate([blk[which][_head_rows(h, n_tok, N_HEADS), :].astype(BF16) for blk in blocks],
                                   axis=0)

        ss = []
        for ci, (_, blocks) in enumerate(chains):
            cum = jnp.concatenate([cums[ci, bi] for bi in range(len(blocks))], axis=1)
            s = jnp.concatenate([_dot_nt(qs[h], head_rows(blocks, 0, h)) - cum[h:h + 1, :] for h in range(N_HEADS)],
                                axis=0)
            ss.append(s if mask is None else jnp.where(mask, s, NEG))
        for (st, blocks), s in zip(chains, ss):
            def pv(pb, blocks=blocks):
                return jnp.concatenate([_dot(pb[h * dl:(h + 1) * dl], head_rows(blocks, 1, h))
                                        for h in range(N_HEADS)], axis=0)
            m_ref[st], l_ref[st], acc_ref[st] = _softmax_step(s, m_ref[st], l_ref[st], acc_ref[st], pv)

    per_chain = pps // n_chain
    attend([(ci, [(k_refs[r], v_refs[r], lf_refs[r][...]) for r in range(ci * per_chain, (ci + 1) * per_chain)])
            for ci in range(n_chain)], PAGE, None)

    @pl.when(p == n_steps - 1)
    def _():
        causal = _iota((nr, dl), 1) <= _iota((nr, dl), 0) % dl
        attend([(0, [(kn_ref, vn_ref, lfn_ref[0])])], dl, causal)
        m = m_ref[0]
        for ci in range(1, n_chain):
            m = jnp.maximum(m, m_ref[ci])
        l, acc = jnp.zeros((nr, 1), F32), jnp.zeros((nr, HEAD_DIM), F32)
        for ci in range(n_chain):
            w = jnp.exp(m_ref[ci] - m)
            l, acc = l + w * l_ref[ci], acc + w * acc_ref[ci]
        out = acc / l
        for h in range(N_HEADS):
            o_ref[0, :, h * HEAD_DIM:(h + 1) * HEAD_DIM] = out[h * dl:(h + 1) * dl].astype(BF16)


def _fox_attn_decode(q, kn, vn, lfn, cache_k, cache_v, cache_lf, layer, pt_flat, n_pages, pps):
    DB, DL, _ = q.shape
    n_steps = n_pages // pps
    nr = N_HEADS * DL
    n_chain = max(pps // 4, 1)
    lfn_t =jnp.pad(jnp.swapaxes(lfn, 1, 2), ((0, 0), (0, 0), (0, LANES - DL)))

    def page_spec(rows, r):
        return pl.BlockSpec((None, None, rows, LANES),
                            lambda b, p, pt, r=r: (layer, pt[b * n_pages + p * pps + r], 0, 0))

    new_rows = pl.BlockSpec((DL * N_HEADS, HEAD_DIM), lambda b, p, pt: (b, 0))
    tok = lambda n, w: pl.BlockSpec((1, n, w), lambda b, p, pt: (b, 0, 0))
    in_specs = [tok(DL, BRANCH), new_rows, new_rows, tok(N_HEADS, LANES)]
    in_specs += [page_spec(PAGE * N_HEADS, r) for r in range(pps)] * 2 + [page_spec(N_HEADS, r) for r in range(pps)]
    grid_spec = pltpu.PrefetchScalarGridSpec(
        num_scalar_prefetch=1, grid=(DB, n_steps), in_specs=in_specs, out_specs=tok(DL, BRANCH),
        scratch_shapes=[pltpu.VMEM((n_chain, nr, 1), F32), pltpu.VMEM((n_chain, nr, 1), F32),
                        pltpu.VMEM((n_chain, nr, HEAD_DIM), F32), pltpu.VMEM((N_HEADS, LANES), F32)])
    return pl.pallas_call(
        functools.partial(_fox_dec_kernel, pps=pps, n_steps=n_steps, dl=DL),
        grid_spec=grid_spec,
        out_shape=jax.ShapeDtypeStruct((DB, DL, BRANCH), BF16),
        compiler_params=_params("parallel", "arbitrary"),
    )(pt_flat, q, kn, vn, lfn_t, *([cache_k] * pps), *([cache_v] * pps), *([cache_lf] * pps))


def _key_to_float(t):
    return pltpu.bitcast(t ^ ((t >> 31) & 0x7FFFFFFF), F32)


def _kth_largest(count_ge, shape, n_sel, bits_per_step):
    def step(si, t):
        shift = 32 - bits_per_step * (si + 1)
        best = t
        for j in range(1, 2 ** bits_per_step):
            cand = t ^ lax.shift_left(jnp.int32(j), shift)
            best = jnp.where(count_ge(_key_to_float(cand)) >= n_sel, jnp.maximum(best, cand), best)
        return best
    t = lax.fori_loop(0, 32 // bits_per_step, step, jnp.full(shape, INT_MIN, jnp.int32))
    return _key_to_float(jnp.maximum(t, NEG_INF_KEY + 1))


def _strict_upper(n):
    return (_iota((n, n), 0) < _iota((n, n), 1)).astype(BF16)


def _dsa_attn_kernel(q_ref, qi_ref, wi_ref, k_ref, v_ref, ki_ref, o_ref, sc_ref, wb_ref, *, n_sel, tq, W):
    i = pl.program_id(1)
    nch = (i * tq + tq + W - 1) // W
    reps = W // LANES
    wi = wi_ref[0]
    for h in range(IDX_HEADS):
        wb_ref[h] = jnp.broadcast_to(wi[:, h:h + 1], (tq, LANES))
    row, col = _iota((tq, W), 0), _iota((tq, W), 1)

    def fill(c, _):
        kc = ki_ref[0, pl.ds(pl.multiple_of(c * W, W), W), :].astype(BF16)
        sc = jnp.zeros((tq, W), F32)
        for h in range(IDX_HEADS):
            d = _dot_nt(qi_ref[0, :, h * IDX_DIM:(h + 1) * IDX_DIM], kc)
            sc = sc + jnp.concatenate([wb_ref[h]] * reps, axis=1) * jnp.maximum(d, 0.0)
        sc_ref[c] = jnp.where(c * W + col <= i * tq + row, sc, -jnp.inf)
        return 0

    lax.fori_loop(0, nch, fill, 0)

    def count(pred):
        def body(c, acc):
            x = jnp.where(pred(sc_ref[c]), 1.0, 0.0)
            for r in range(reps):
                acc = acc + x[:, r * LANES:(r + 1) * LANES]
            return acc
        return jnp.sum(lax.fori_loop(0, nch, body, jnp.zeros((tq, LANES), F32)), axis=-1, keepdims=True)

    thr = _kth_largest(lambda f: count(lambda s: s >= f), (tq, 1), n_sel, 1)

    @pl.when(jnp.max(count(lambda s: s >= thr)) > n_sel)
    def _():
        need = n_sel - count(lambda s: s > thr)
        upper = _strict_upper(W)

        def body(c, seen):
            sc = sc_ref[c]
            eq = sc == thr
            eqf = jnp.where(eq, 1.0, 0.0)
            rank = seen + _dot(eqf.astype(BF16), upper)
            sc_ref[c] = jnp.where(eq & (rank >= need), -jnp.inf, sc)
            return seen + jnp.sum(eqf, axis=-1, keepdims=True)

        lax.fori_loop(0, nch, body, jnp.zeros((tq, 1), F32))

    qgs = [jnp.concatenate(
        [q_ref[0, :, (g * DSA_GROUP + r) * HEAD_DIM:(g * DSA_GROUP + r + 1) * HEAD_DIM] for r in range(DSA_GROUP)],
        axis=0) for g in range(DSA_KV_HEADS)]

    def body(c, carry):
        start = pl.multiple_of(c * W, W)
        bias = jnp.where(sc_ref[c] >= thr, 0.0, NEG)[None]
        out = []
        for g in range(DSA_KV_HEADS):
            m, l, acc = carry[g]
            gsl = slice(g * HEAD_DIM, (g + 1) * HEAD_DIM)
            s = _dot_nt(qgs[g], k_ref[0, pl.ds(start, W), gsl]).reshape(DSA_GROUP, tq, W) + bias
            m_new = jnp.maximum(m, jnp.max(s, axis=-1, keepdims=True))
            alpha = jnp.exp(m - m_new)
            p = jnp.exp(s - m_new)
            l = alpha * l + jnp.sum(p, axis=-1, keepdims=True)
            pv = _dot(p.reshape(DSA_GROUP * tq, W).astype(BF16), v_ref[0, pl.ds(start, W), gsl])
            out.append((m_new, l, alpha * acc + pv.reshape(DSA_GROUP, tq, HEAD_DIM)))
        return tuple(out)

    init = tuple((jnp.full((DSA_GROUP, tq, 1), NEG, F32), jnp.zeros((DSA_GROUP, tq, 1), F32),
                  jnp.zeros((DSA_GROUP, tq, HEAD_DIM), F32)) for _ in range(DSA_KV_HEADS))
    res = lax.fori_loop(0, nch, body, init)
    for g, (_, l, acc) in enumerate(res):
        out = acc / l
        for r in range(DSA_GROUP):
            hh = g * DSA_GROUP + r
            o_ref[0, :, hh * HEAD_DIM:(hh + 1) * HEAD_DIM] = out[r].astype(BF16)


def _dsa_attn_prompt(q, qi, wi, k, v, ki, n_sel, tq):
    B, L, _ = q.shape
    nb = L // tq
    W = min(4 * LANES, L)
    blk = lambda w: pl.BlockSpec((1, tq, w), lambda b, i: (b, i, 0))
    full = lambda w: pl.BlockSpec((1, L, w), lambda b, i: (b, 0, 0))
    return pl.pallas_call(
        functools.partial(_dsa_attn_kernel, n_sel=n_sel, tq=tq, W=W),
        grid=(B, nb),
        in_specs=[blk(BRANCH), blk(IDX_HEADS * IDX_DIM), blk(IDX_HEADS), full(DSA_KV), full(DSA_KV), full(IDX_DIM)],
        out_specs=blk(BRANCH),
        out_shape=jax.ShapeDtypeStruct((B, L, BRANCH), BF16),
        scratch_shapes=[pltpu.VMEM((L // W, tq, W), F32), pltpu.VMEM((IDX_HEADS, tq, LANES), F32)],
        compiler_params=_params("parallel", "arbitrary"),
    )(q, qi, wi, k, v, ki)


def _dsa_dec_kernel(pt_ref, q_ref, qi_ref, wi_ref, kn_ref, vn_ref, kin_ref, *rest, pps, n_pages, dl, n_sel, bb):
    n_pg = bb * pps
    k_refs, v_refs, ki_refs = rest[0:n_pg], rest[n_pg:2 * n_pg], rest[2 * n_pg:3 * n_pg]
    o_ref = rest[3 * n_pg]
    ks_ref, vs_ref, kis_ref = rest[3 * n_pg + 1:]
    p = pl.program_id(1)
    past = n_pages * PAGE
    nk = past + PAGE
    nr = bb * dl
    G = DSA_KV_HEADS

    for e in range(bb):
        for r in range(pps):
            page = p * pps + r
            start = pl.multiple_of(page * (PAGE * G), PAGE * G)
            ks_ref[e, pl.ds(start, PAGE * G), :] = k_refs[e * pps + r][...]
            vs_ref[e, pl.ds(start, PAGE * G), :] = v_refs[e * pps + r][...]
            kis_ref[e, page] = ki_refs[e * pps + r][...]

    @pl.when(p == n_pages // pps - 1)
    def _():
        for e in range(bb):
            for ref, new in ((ks_ref, kn_ref), (vs_ref, vn_ref)):
                ref[e, past * G:(past + dl) * G, :] = new[e * dl * G:(e + 1) * dl * G, :]
                ref[e, (past + dl) * G:nk * G, :] = jnp.zeros(((PAGE - dl) * G, HEAD_DIM), F32)
            kis_ref[e, n_pages] = kin_ref[e]

        rows = []
        for e in range(bb):
            qi, wi = qi_ref[e], wi_ref[e]
            chunks = []
            for c in range(n_pages + 1):
                wr = wi * jnp.maximum(_dot(qi, kis_ref[e, c].astype(BF16)), 0.0)
                sc = wr[0:dl]
                for h in range(1, IDX_HEADS):
                    sc = sc + wr[h * dl:(h + 1) * dl]
                chunks.append(sc)
            rows.append(jnp.concatenate(chunks, axis=1))
        score = jnp.concatenate(rows, axis=0)
        valid = _iota((nr, nk), 1) <= past + _iota((nr, nk), 0) % dl
        score = jnp.where(valid, score, -jnp.inf)

        def count(mask):
            return jnp.sum(jnp.where(mask, 1.0, 0.0), axis=-1, keepdims=True)

        thr = _kth_largest(lambda f: count(score >= f), (nr, 1), n_sel, 2)

        def drop_surplus_ties(sc):
            need = n_sel - count(sc > thr)
            upper = _strict_upper(PAGE)
            seen, kept = jnp.zeros((nr, 1), F32), []
            for c in range(n_pages + 1):
                scc = sc[:, c * PAGE:(c + 1) * PAGE]
                eq = scc == thr
                eqf = jnp.where(eq, 1.0, 0.0)
                rank = seen + _dot(eqf.astype(BF16), upper)
                kept.append(jnp.where(eq & (rank >= need), -jnp.inf, scc))
                seen = seen + jnp.sum(eqf, axis=-1, keepdims=True)
            return jnp.concatenate(kept, axis=1)

        score = lax.cond(jnp.max(count(score >= thr)) > n_sel, drop_surplus_ties, lambda sc: sc, score)
        sel = score >= thr

        units = [(e, g) for e in range(bb) for g in range(G)]

        def stage(fn):
            return [fn(i, e, g) for i, (e, g) in enumerate(units)]

        qg = stage(lambda i, e, g: jnp.concatenate(
            [q_ref[e, :, (g * DSA_GROUP + r) * HEAD_DIM:(g * DSA_GROUP + r + 1) * HEAD_DIM].astype(F32)
             for r in range(DSA_GROUP)], axis=0).astype(BF16))
        s = stage(lambda i, e, g: jnp.where(
            sel[e * dl:(e + 1) * dl][None],
            _dot_nt(qg[i], ks_ref[e, _head_rows(g, nk, G), :].astype(BF16)).reshape(DSA_GROUP, dl, nk), NEG))
        pr = stage(lambda i, e, g: jnp.exp(s[i] - jnp.max(s[i], axis=-1, keepdims=True)))
        den = stage(lambda i, e, g: jnp.sum(pr[i], axis=-1, keepdims=True))
        pv = stage(lambda i, e, g: _dot(pr[i].reshape(DSA_GROUP * dl, nk).astype(BF16),
                                        vs_ref[e, _head_rows(g, nk, G), :].astype(BF16)))
        for i, (e, g) in enumerate(units):
            out = pv[i].reshape(DSA_GROUP, dl, HEAD_DIM) / den[i]
            for r in range(DSA_GROUP):
                hh = g * DSA_GROUP + r
                o_ref[e, :, hh * HEAD_DIM:(hh + 1) * HEAD_DIM] = out[r].astype(BF16)


def _dsa_attn_decode(q, qi, wi, kn, vn, kin, cache_k, cache_v, cache_ki, layer, pt_flat, n_pages, pps, n_sel):
    DB, DL, _ = q.shape
    nk = n_pages * PAGE + PAGE
    G = DSA_KV_HEADS
    bb = max(d for d in (1, 2, 4) if DB % d == 0)
    qi_hq = jnp.swapaxes(qi.reshape(DB, DL, IDX_HEADS, IDX_DIM), 1, 2).reshape(DB, IDX_HEADS * DL, IDX_DIM)
    wi_hq = jnp.swapaxes(wi.reshape(DB, DL, IDX_HEADS), 1, 2).reshape(DB, IDX_HEADS * DL, 1)
    kin_t = jnp.pad(jnp.swapaxes(kin, 1, 2), ((0, 0), (0, 0), (0, PAGE - DL)))

    def page_specs(rows):
        return [pl.BlockSpec((None, None, rows, LANES),
                             lambda b, p, pt, e=e, r=r: (layer, pt[(b * bb + e) * n_pages + p * pps + r], 0, 0))
                for e in range(bb) for r in range(pps)]

    tok = lambda n, w: pl.BlockSpec((bb, n, w), lambda b, p, pt: (b, 0, 0))
    new_rows = pl.BlockSpec((bb * DL * G, HEAD_DIM), lambda b, p, pt: (b, 0))
    in_specs = [tok(DL, BRANCH), tok(IDX_HEADS * DL, IDX_DIM), tok(IDX_HEADS * DL, 1),
                new_rows, new_rows, tok(IDX_DIM, PAGE)]
    in_specs += page_specs(PAGE * G) * 2 + page_specs(IDX_DIM)
    n_pg = bb * pps
    grid_spec = pltpu.PrefetchScalarGridSpec(
        num_scalar_prefetch=1, grid=(DB // bb, n_pages // pps), in_specs=in_specs, out_specs=tok(DL, BRANCH),
        scratch_shapes=[pltpu.VMEM((bb, nk * G, HEAD_DIM), F32), pltpu.VMEM((bb, nk * G, HEAD_DIM), F32),
                        pltpu.VMEM((bb, n_pages + 1, IDX_DIM, PAGE), F32)])
    return pl.pallas_call(
        functools.partial(_dsa_dec_kernel, pps=pps, n_pages=n_pages, dl=DL, n_sel=n_sel, bb=bb),
        grid_spec=grid_spec,
        out_shape=jax.ShapeDtypeStruct((DB, DL, BRANCH), BF16),
        compiler_params=_params("parallel", "arbitrary"),
    )(pt_flat, q, qi_hq, wi_hq, kn, vn, kin_t, *([cache_k] * n_pg), *([cache_v] * n_pg), *([cache_ki] * n_pg))


def _gdn_kernel(x_ref, z_ref, g_ref, b_ref, cw_ref, nw_ref, cs_ref, s0_ref, o_ref, sout_ref, xb_ref, s_ref,
                *, C, nc, bb):
    c = pl.program_id(1)
    HIST = SUBLANES

    @pl.when(c == 0)
    def _():
        xb_ref[:, 0:HIST, :] = cs_ref[...]
        s_ref[...] = s0_ref[...]

    incl, strict = _tril(C), _tril(C, strict=True)
    eye_c = (_iota((C, C), 0) == _iota((C, C), 1)).astype(F32)
    eye_h = (_iota((N_HEADS, N_HEADS), 0) == _iota((N_HEADS, N_HEADS), 1)).astype(F32)
    n_dbl = int(math.log2(C)) - 1

    convs, G_alls, Gt_alls = [], [], []
    for e in range(bb):
        xb_ref[e, HIST:HIST + C, :] = x_ref[e]
        conv = xb_ref[e, HIST:HIST + C, :] * cw_ref[CONV_W - 1:CONV_W, :]
        for j in range(CONV_W - 1):
            off = HIST - (CONV_W - 1) + j
            conv = conv + xb_ref[e, off:off + C, :] * cw_ref[j:j + 1, :]
        hist = xb_ref[e, C:C + HIST, :]
        xb_ref[e, 0:HIST, :] = hist
        convs.append(_silu(conv))
        G = _dot(incl.astype(F32), g_ref[e], HIGHEST)
        G_alls.append(G)
        Gt_alls.append(_dot_nt(eye_h, G, HIGHEST))

    units = [(e, h) for e in range(bb) for h in range(N_HEADS)]

    def stage(fn):
        return [fn(i, e, h) for i, (e, h) in enumerate(units)]

    def head(e, j, h):
        return convs[e][:, j * BRANCH + h * HEAD_DIM:j * BRANCH + (h + 1) * HEAD_DIM]

    def l2n(a):
        return a * lax.rsqrt(jnp.sum(a * a, axis=-1, keepdims=True) + EPS)

    q = stage(lambda i, e, h: l2n(head(e, 0, h)) * ATTN_SCALE)
    k = stage(lambda i, e, h: l2n(head(e, 1, h)))
    v = stage(lambda i, e, h: head(e, 2, h))
    Gc = stage(lambda i, e, h: G_alls[e][:, h:h + 1])
    bc = stage(lambda i, e, h: b_ref[e][:, h:h + 1])
    decay = stage(lambda i, e, h: jnp.exp(jnp.where(incl, Gc[i] - Gt_alls[e][h:h + 1, :], -jnp.inf)))
    kk = stage(lambda i, e, h: _dot1(k[i], k[i], _dot_nt))
    A = stage(lambda i, e, h: jnp.where(strict, bc[i] * kk[i] * decay[i], 0.0))
    T = stage(lambda i, e, h: eye_c - A[i])
    P = stage(lambda i, e, h: _dot1(A[i], A[i]))
    for it in range(n_dbl):
        TP = stage(lambda i, e, h: _dot1(T[i], P[i]))
        T = stage(lambda i, e, h: T[i] + TP[i])
        if it + 1 < n_dbl:
            P = stage(lambda i, e, h: _dot1(P[i], P[i]))
    eG = stage(lambda i, e, h: jnp.exp(Gc[i]))
    WU = stage(lambda i, e, h: _dot1(T[i], jnp.concatenate([k[i] * (bc[i] * eG[i]), v[i] * bc[i]], axis=1)))
    Aqk = stage(lambda i, e, h: _dot1(q[i], k[i], _dot_nt) * decay[i])
    G_last = stage(lambda i, e, h: Gc[i][C - 1:C, :])
    k_dec = stage(lambda i, e, h: k[i] * jnp.exp(G_last[i] - Gc[i]))
    S = stage(lambda i, e, h: s_ref[e, h])
    WqS = stage(lambda i, e, h: _dot1(jnp.concatenate([WU[i][:, 0:HEAD_DIM], q[i]], axis=0), S[i]))
    Vn = stage(lambda i, e, h: WU[i][:, HEAD_DIM:2 * HEAD_DIM] - WqS[i][0:C])
    o = stage(lambda i, e, h: eG[i] * WqS[i][C:2 * C] + _dot1(Aqk[i], Vn[i]))
    S_new = stage(lambda i, e, h: jnp.exp(G_last[i]) * S[i] + _dot1(k_dec[i], Vn[i], _dot_tn))
    for i, (e, h) in enumerate(units):
        sl = slice(h * HEAD_DIM, (h + 1) * HEAD_DIM)
        s_ref[e, h] = S_new[i]
        on = o[i] * lax.rsqrt(jnp.mean(o[i] * o[i], axis=-1, keepdims=True) + EPS) * nw_ref[...]
        o_ref[e, :, sl] = (on * _silu(z_ref[e, :, sl].astype(F32))).astype(BF16)

    @pl.when(c == nc - 1)
    def _():
        sout_ref[...] = s_ref[...]


def _gdn_mix(qkv, z, g, beta, conv_w, norm_w, conv_state, S0):
    B, L, _ = qkv.shape
    C = min(GDN_CHUNK, L)
    nc = L // C
    assert C & (C - 1) == 0 and L % C == 0
    bb = max(d for d in (1, 2, 4) if B % d == 0 and d * C <= GDN_CHUNK)
    cs = jnp.pad(conv_state, ((0, 0), (SUBLANES - (CONV_W - 1), 0), (0, 0)))
    blk = lambda w: pl.BlockSpec((bb, C, w), lambda b, c: (b, c, 0))
    state = pl.BlockSpec((bb, N_HEADS, HEAD_DIM, HEAD_DIM), lambda b, c: (b, 0, 0, 0))
    return pl.pallas_call(
        functools.partial(_gdn_kernel, C=C, nc=nc, bb=bb),
        grid=(B // bb, nc),
        in_specs=[blk(3 * BRANCH), blk(BRANCH), blk(N_HEADS), blk(N_HEADS),
                  pl.BlockSpec((CONV_W, 3 * BRANCH), lambda b, c: (0, 0)),
                  pl.BlockSpec((1, HEAD_DIM), lambda b, c: (0, 0)),
                  pl.BlockSpec((bb, SUBLANES, 3 * BRANCH), lambda b, c: (b, 0, 0)),
                  state],
        out_specs=[blk(BRANCH), state],
        out_shape=[jax.ShapeDtypeStruct((B, L, BRANCH), BF16),
                   jax.ShapeDtypeStruct((B, N_HEADS, HEAD_DIM, HEAD_DIM), F32)],
        scratch_shapes=[pltpu.VMEM((bb, C + SUBLANES, 3 * BRANCH), F32),
                        pltpu.VMEM((bb, N_HEADS, HEAD_DIM, HEAD_DIM), F32)],
        compiler_params=_params("parallel", "arbitrary"),
    )(qkv, z, g, beta, conv_w, norm_w.reshape(1, HEAD_DIM), cs, S0)


def _pad_cols(w, n=LANES):
    return jnp.pad(w, ((0, 0), (0, n - w.shape[1])))


def _row_tile(T):
    return 256 if T % 256 == 0 else T


def kernel(x_prompt, x_sample, cache_fox_k, cache_fox_v, cache_fox_logf, cache_dsa_k, cache_dsa_v, cache_dsa_kidx, state_gdn_conv, state_gdn_S, page_table, norm_w, norm_f, fox_w_in, fox_b_f, fox_w_out, dsa_w_in, dsa_w_out, gdn_w_in, gdn_conv_w, gdn_A_log, gdn_dt_bias, gdn_norm_w, gdn_w_out):
    B, L, _ = x_prompt.shape
    DB, DL, _ = x_sample.shape
    n_pages = page_table.shape[1]
    past = n_pages * PAGE
    depth = norm_w.shape[0]
    pt_flat = page_table.reshape(-1)
    pages_per_step = lambda want: max(p for p in (1, 2, 4, 8, 16) if p <= want and n_pages % p == 0)
    fox_pps, dsa_pps = pages_per_step(16), pages_per_step(4)
    groups = ((B, L), (DB, DL))
    xs = [x_prompt.reshape(B * L, D_MODEL), x_sample.reshape(DB * DL, D_MODEL)]
    tms = [_row_tile(nb * nl) for nb, nl in groups]
    pos = [jnp.arange(L), past + jnp.arange(DL)]
    tq_p = max(t for t in (PAGE, 2 * PAGE, 4 * PAGE) if L % t == 0)

    pool = cache_fox_k.shape[1]
    fox_ck = cache_fox_k.reshape(-1, pool, PAGE * N_HEADS, HEAD_DIM)
    fox_cv = cache_fox_v.reshape(-1, pool, PAGE * N_HEADS, HEAD_DIM)
    fox_clf = jnp.swapaxes(cache_fox_logf, 2, 3)
    dsa_ck = cache_dsa_k.reshape(-1, pool, PAGE * DSA_KV_HEADS, HEAD_DIM)
    dsa_cv = cache_dsa_v.reshape(-1, pool, PAGE * DSA_KV_HEADS, HEAD_DIM)
    dsa_cki = jnp.swapaxes(cache_dsa_kidx, 2, 3)

    st = {name: ([], []) for name in ("fox_k", "fox_v", "fox_lf", "dsa_k", "dsa_v", "dsa_ki", "gdn_conv", "gdn_S")}
    finals = [None, None]

    for i in range(depth):
        kind, j = i % 3, i // 3
        nw = norm_w[i].reshape(1, D_MODEL)
        last = i == depth - 1
        for gi, (nb, nl) in enumerate(groups):
            x2d, tm = xs[gi], tms[gi]
            T = nb * nl
            prompt = gi == 0
            sh = lambda a: a.reshape(nb, nl, a.shape[-1])
            if kind == 0:
                w = fox_w_in[j]
                w_main = jnp.concatenate([w[:, :3 * BRANCH], w[:, 3 * BRANCH + N_HEADS:]], axis=1).astype(BF16)
                w_small = _pad_cols(w[:, 3 * BRANCH:3 * BRANCH + N_HEADS]).astype(BF16)
                outs = [(1, BRANCH, BF16), (N_HEADS, HEAD_DIM, F32), (N_HEADS, HEAD_DIM, F32), (1, BRANCH, BF16),
                        (1, N_HEADS, F32)]
                if prompt:
                    outs += [(1, BRANCH, BF16), (1, BRANCH, BF16)]
                res = _in_proj(_fox_in_kernel, x2d, [nw, w_main, w_small, fox_b_f[j].reshape(1, N_HEADS)], [], outs, tm)
                q, k, v, g, lf = res[:5]
                if prompt:
                    o = _fox_attn_prompt(sh(q), sh(res[5]), sh(res[6]), _cumsum_seq(sh(lf)), tq_p, tq_p, tq_p)
                else:
                    o = _fox_attn_decode(sh(q), k, v, sh(lf), fox_ck, fox_cv, fox_clf, j, pt_flat, n_pages, fox_pps)
                st["fox_k"][gi].append(k.reshape(nb, nl, N_HEADS, HEAD_DIM))
                st["fox_v"][gi].append(v.reshape(nb, nl, N_HEADS, HEAD_DIM))
                st["fox_lf"][gi].append(lf.reshape(nb, nl, N_HEADS))
                w_out = fox_w_out[j]
            elif kind == 1:
                w = dsa_w_in[j]
                o_qi = BRANCH + 2 * DSA_KV
                o_wi = o_qi + IDX_HEADS * IDX_DIM
                o_ki = o_wi + IDX_HEADS
                o_g = o_ki + IDX_DIM
                w_main = jnp.concatenate([w[:, :o_wi], w[:, o_g:]], axis=1).astype(BF16)
                w_small = _pad_cols(jnp.concatenate([w[:, o_ki:o_g], w[:, o_wi:o_ki]], axis=1)).astype(BF16)
                tables = list(_rope_tables(pos[gi], HEAD_DIM, tm) + _rope_tables(pos[gi], IDX_DIM, tm))
                outs = [(1, BRANCH, BF16), (DSA_KV_HEADS, HEAD_DIM, F32), (DSA_KV_HEADS, HEAD_DIM, F32),
                        (1, IDX_HEADS * IDX_DIM, BF16), (1, BRANCH, BF16), (1, IDX_DIM, F32), (1, IDX_HEADS, F32)]
                if prompt:
                    outs += [(1, DSA_KV, BF16), (1, DSA_KV, BF16)]
                res = _in_proj(_dsa_in_kernel, x2d, [nw, w_main, w_small], tables, outs, tm)
                q, k, v, qi, g, ki, wi = res[:7]
                if prompt:
                    o = _dsa_attn_prompt(sh(q), sh(qi), sh(wi), sh(res[7]), sh(res[8]), sh(ki),
                                         min(TOPK_MAX, L // 4), PAGE)
                else:
                    o = _dsa_attn_decode(sh(q), sh(qi), sh(wi), k, v, sh(ki), dsa_ck, dsa_cv, dsa_cki,
                                         j, pt_flat, n_pages, dsa_pps, min(TOPK_MAX, (past + DL) // 4))
                st["dsa_k"][gi].append(k.reshape(nb, nl, DSA_KV_HEADS, HEAD_DIM))
                st["dsa_v"][gi].append(v.reshape(nb, nl, DSA_KV_HEADS, HEAD_DIM))
                st["dsa_ki"][gi].append(ki.reshape(nb, nl, IDX_DIM))
                w_out = dsa_w_out[j]
            else:
                w = gdn_w_in[j]
                w_main = jnp.concatenate([w[:, :3 * BRANCH], w[:, 3 * BRANCH + 2 * N_HEADS:]], axis=1).astype(BF16)
                w_small = _pad_cols(w[:, 3 * BRANCH:3 * BRANCH + 2 * N_HEADS]).astype(BF16)
                qkv, z, gg, beta = _in_proj(
                    _gdn_in_kernel, x2d,
                    [nw, w_main, w_small, gdn_A_log[j].reshape(1, N_HEADS), gdn_dt_bias[j].reshape(1, N_HEADS)], [],
                    [(1, 3 * BRANCH, F32), (1, BRANCH, BF16), (1, N_HEADS, F32), (1, N_HEADS, F32)], tm)
                if prompt:
                    conv0 = jnp.zeros((nb, CONV_W - 1, 3 * BRANCH), F32)
                    S0 = jnp.zeros((nb, N_HEADS, HEAD_DIM, HEAD_DIM), F32)
                else:
                    conv0, S0 = state_gdn_conv[j], state_gdn_S[j]
                o, S_new = _gdn_mix(sh(qkv), sh(z), sh(gg), sh(beta), gdn_conv_w[j], gdn_norm_w[j], conv0, S0)
                tail = jnp.concatenate([conv0, sh(qkv)[:, -min(nl, CONV_W - 1):]], axis=1)
                st["gdn_conv"][gi].append(tail[:, -(CONV_W - 1):])
                st["gdn_S"][gi].append(S_new)
                g = None
                w_out = gdn_w_out[j]
            y = _out_proj(o.reshape(T, BRANCH), g, x2d, w_out.astype(BF16), norm_f if last else None, tm)
            if last:
                finals[gi] = y.reshape(nb, nl, D_MODEL)
            else:
                xs[gi] = y

    stk = lambda name, gi: jnp.stack(st[name][gi], axis=0)
    return (finals[0], finals[1],
            stk("fox_k", 0), stk("fox_v", 0), stk("fox_lf", 0),
            stk("fox_k", 1), stk("fox_v", 1), stk("fox_lf", 1),
            stk("dsa_k", 0), stk("dsa_v", 0), stk("dsa_ki", 0),
            stk("dsa_k", 1), stk("dsa_v", 1), stk("dsa_ki", 1),
            stk("gdn_conv", 0), stk("gdn_S", 0),
            stk("gdn_conv", 1), stk("gdn_S", 1))
```

```python
import functools
import math

import jax
import jax.numpy as jnp
from jax import lax
from jax.experimental import pallas as pl
from jax.experimental.pallas import tpu as pltpu

F32 = jnp.float32
BF16 = jnp.bfloat16
HIGHEST = lax.Precision.HIGHEST

D_MODEL = 1024
N_HEADS = 8
HEAD_DIM = 128
BRANCH = N_HEADS * HEAD_DIM
ATTN_SCALE = HEAD_DIM ** -0.5
DSA_KV_HEADS = 2
DSA_GROUP = N_HEADS // DSA_KV_HEADS
DSA_KV = DSA_KV_HEADS * HEAD_DIM
IDX_HEADS = 8
IDX_DIM = 64
IDX_SCALE = IDX_DIM ** -0.5
TOPK_MAX = 256
CONV_W = 4
GDN_CHUNK = 64
ROPE_THETA = 10000.0
EPS = 1e-6
PAGE = 128
LANES = 128
SUBLANES = 8
NEG = -1e30
INT_MIN = -(2 ** 31)
NEG_INF_KEY = INT_MIN + 0x007FFFFF
V7X_VMEM_BYTES = 64 * 1024 * 1024
VMEM_LIMIT = V7X_VMEM_BYTES * 7 // 8


def _dot(a, b, prec=None):
    return jnp.dot(a, b, preferred_element_type=F32, precision=prec)


def _dot_nt(a, b, prec=None):
    return lax.dot_general(a, b, (((1,), (1,)), ((), ())), preferred_element_type=F32, precision=prec)


def _dot_tn(a, b, prec=None):
    return lax.dot_general(a, b, (((0,), (0,)), ((), ())), preferred_element_type=F32, precision=prec)


def _dot1(a, b, dot=_dot):
    return dot(a.astype(BF16), b.astype(BF16))


def _iota(shape, axis):
    return lax.broadcasted_iota(jnp.int32, shape, axis)


def _tril(n, strict=False):
    r, c = _iota((n, n), 0), _iota((n, n), 1)
    return (r > c) if strict else (r >= c)


def _sigmoid(x):
    return 1.0 / (1.0 + jnp.exp(-x))


def _silu(x):
    return x * _sigmoid(x)


def _softplus(x):
    return jnp.maximum(x, 0.0) + jnp.log1p(jnp.exp(-jnp.abs(x)))


def _params(*sem):
    return pltpu.CompilerParams(dimension_semantics=sem, vmem_limit_bytes=VMEM_LIMIT)


def _head_rows(h, n_tok, n_heads):
    return pl.ds(h, n_tok, stride=n_heads)


def _normed(x_ref, nw_ref):
    x = x_ref[...]
    ms = jnp.mean(x * x, axis=-1, keepdims=True)
    return (x * lax.rsqrt(ms + EPS) * nw_ref[...]).astype(BF16)


def _fox_in_kernel(x_ref, nw_ref, w_ref, ws_ref, bf_ref, q_ref, k_ref, v_ref, g_ref, lf_ref, *copies):
    tm = x_ref.shape[0]
    h = _normed(x_ref, nw_ref)
    q_ref[...] = (_dot(h, w_ref[:, 0:BRANCH]) * ATTN_SCALE).astype(BF16)
    for j, ref in ((1, k_ref), (2, v_ref)):
        for hh in range(N_HEADS):
            col = j * BRANCH + hh * HEAD_DIM
            r = _dot(h, w_ref[:, col:col + HEAD_DIM])
            ref[_head_rows(hh, tm, N_HEADS), :] = r
            if copies:
                copies[j - 1][:, hh * HEAD_DIM:(hh + 1) * HEAD_DIM] = r.astype(BF16)
    g_ref[...] = _dot(h, w_ref[:, 3 * BRANCH:4 * BRANCH]).astype(BF16)
    f = _dot(h, ws_ref[...])[:, 0:N_HEADS] + bf_ref[...]
    lf_ref[...] = -_softplus(-f)


def _rope128(x, cos, sin_signed):
    return x * cos + pltpu.roll(x, HEAD_DIM // 2, 1) * sin_signed


def _rope64(x, cos, sin_signed):
    lane = _iota(x.shape, 1)
    rot = jnp.where((lane & (IDX_DIM - 1)) < IDX_DIM // 2,
                    pltpu.roll(x, LANES - IDX_DIM // 2, 1), pltpu.roll(x, IDX_DIM // 2, 1))
    return x * cos + rot * sin_signed


def _dsa_in_kernel(x_ref, nw_ref, w_ref, ws_ref, c128_ref, s128_ref, c64_ref, s64_ref,
                   q_ref, k_ref, v_ref, qi_ref, g_ref, ki_ref, wi_ref, *copies):
    tm = x_ref.shape[0]
    h = _normed(x_ref, nw_ref)
    c128, s128, c64, s64 = c128_ref[...], s128_ref[...], c64_ref[...], s64_ref[...]
    o_k, o_v, o_qi, o_g = BRANCH, BRANCH + DSA_KV, BRANCH + 2 * DSA_KV, BRANCH + 2 * DSA_KV + IDX_HEADS * IDX_DIM
    for hh in range(N_HEADS):
        sl = slice(hh * HEAD_DIM, (hh + 1) * HEAD_DIM)
        q_ref[:, sl] = (_rope128(_dot(h, w_ref[:, sl]), c128, s128) * ATTN_SCALE).astype(BF16)
    for hh in range(DSA_KV_HEADS):
        sl = slice(hh * HEAD_DIM, (hh + 1) * HEAD_DIM)
        kk = _rope128(_dot(h, w_ref[:, o_k + hh * HEAD_DIM:o_k + (hh + 1) * HEAD_DIM]), c128, s128)
        vv = _dot(h, w_ref[:, o_v + hh * HEAD_DIM:o_v + (hh + 1) * HEAD_DIM])
        k_ref[_head_rows(hh, tm, DSA_KV_HEADS), :] = kk
        v_ref[_head_rows(hh, tm, DSA_KV_HEADS), :] = vv
        if copies:
            copies[0][:, sl] = kk.astype(BF16)
            copies[1][:, sl] = vv.astype(BF16)
    for hh in range(IDX_HEADS * IDX_DIM // LANES):
        sl = slice(hh * LANES, (hh + 1) * LANES)
        qi_ref[:, sl] = _rope64(_dot(h, w_ref[:, o_qi + hh * LANES:o_qi + (hh + 1) * LANES]), c64, s64).astype(BF16)
    g_ref[...] = _dot(h, w_ref[:, o_g:o_g + BRANCH]).astype(BF16)
    small = _dot(h, ws_ref[...])
    ki_ref[...] = _rope64(small, c64, s64)[:, 0:IDX_DIM]
    wi_ref[...] = small[:, IDX_DIM:IDX_DIM + IDX_HEADS] * (IDX_HEADS ** -0.5 * IDX_SCALE)


def _gdn_in_kernel(x_ref, nw_ref, w_ref, ws_ref, alog_ref, dtb_ref, qkv_ref, z_ref, g_ref, beta_ref):
    h = _normed(x_ref, nw_ref)
    for j in range(3):
        sl = slice(j * BRANCH, (j + 1) * BRANCH)
        qkv_ref[:, sl] = _dot(h, w_ref[:, sl])
    z_ref[...] = _dot(h, w_ref[:, 3 * BRANCH:4 * BRANCH]).astype(BF16)
    small = _dot(h, ws_ref[...])
    a = small[:, 0:N_HEADS]
    b = small[:, N_HEADS:2 * N_HEADS]
    g_ref[...] = -jnp.exp(alog_ref[...]) * _softplus(a + dtb_ref[...])
    beta_ref[...] = _sigmoid(b)


def _row_spec(tm, n):
    return pl.BlockSpec((tm, n), lambda i: (i, 0))


def _full_spec(shape):
    return pl.BlockSpec(shape, lambda i: (0,) * len(shape))


def _in_proj(kern, x2d, consts, tables, outs, tm, stacked=None):
    T = x2d.shape[0]
    in_specs = [_row_spec(tm, D_MODEL)] + [_full_spec(c.shape) for c in consts[:3]]
    args = [x2d] + list(consts[:3])
    for t in tables:
        nblk = t.shape[0] // tm
        in_specs.append(pl.BlockSpec((tm, LANES), lambda i, nblk=nblk: (i % nblk, 0)))
        args.append(t)
    for c in consts[3:]:
        in_specs.append(_full_spec(c.shape))
        args.append(c)
    n_in = len(args)
    out_specs = [_row_spec(tm * r, n) for r, n, _ in outs]
    out_shape = [jax.ShapeDtypeStruct((T * r, n), dt) for r, n, dt in outs]
    aliases = {}
    if stacked is not None:
        slot, n_slots, stacks = stacked
        for oi, prev in stacks.items():
            r, n, dt = outs[oi]
            out_specs[oi] = pl.BlockSpec((None, tm * r, n), lambda i: (slot, i, 0))
            out_shape[oi] = jax.ShapeDtypeStruct((n_slots, T * r, n), dt)
            if prev is not None:
                aliases[len(args)] = oi
                in_specs.append(pl.BlockSpec(memory_space=pl.ANY))
                args.append(prev)
    n_alias = len(args) - n_in

    def body(*refs):
        kern(*refs[:n_in], *refs[n_in + n_alias:])

    return pl.pallas_call(
        body,
        grid=(T // tm,),
        in_specs=in_specs,
        out_specs=out_specs,
        out_shape=out_shape,
        input_output_aliases=aliases,
        compiler_params=_params("parallel"),
    )(*args)


def _rope_tables(pos, head_dim, tm):
    half = head_dim // 2
    inv = ROPE_THETA ** (-jnp.arange(half, dtype=F32) / half)
    ang = pos.astype(F32)[:, None] * inv[None, :]
    cos, sin = jnp.cos(ang), jnp.sin(ang)
    reps = LANES // head_dim
    cf = jnp.tile(jnp.concatenate([cos, cos], axis=-1), (1, reps))
    ss = jnp.tile(jnp.concatenate([-sin, sin], axis=-1), (1, reps))
    if cf.shape[0] < tm:
        cf = jnp.tile(cf, (tm // cf.shape[0], 1))
        ss = jnp.tile(ss, (tm // ss.shape[0], 1))
    return cf, ss


def _out_kernel(*refs, gated, final):
    refs = list(refs)
    o_ref = refs.pop(0)
    g_ref = refs.pop(0) if gated else None
    x_ref, w_ref = refs.pop(0), refs.pop(0)
    nf_ref = refs.pop(0) if final else None
    y_ref = refs.pop(0)
    o = o_ref[...]
    if gated:
        o = (o.astype(F32) * _silu(g_ref[...].astype(F32))).astype(BF16)
    y = x_ref[...] + _dot(o, w_ref[...])
    if final:
        ms = jnp.mean(y * y, axis=-1, keepdims=True)
        y = y * lax.rsqrt(ms + EPS) * nf_ref[...]
    y_ref[...] = y


def _out_proj(o, g, x2d, w_bf16, norm_f, tm):
    T = x2d.shape[0]
    gated, final = g is not None, norm_f is not None
    args, specs = [o], [_row_spec(tm, BRANCH)]
    if gated:
        args.append(g)
        specs.append(_row_spec(tm, BRANCH))
    args += [x2d, w_bf16]
    specs += [_row_spec(tm, D_MODEL), _full_spec(w_bf16.shape)]
    if final:
        args.append(norm_f.reshape(1, D_MODEL))
        specs.append(_full_spec((1, D_MODEL)))
    return pl.pallas_call(
        functools.partial(_out_kernel, gated=gated, final=final),
        grid=(T // tm,), in_specs=specs, out_specs=_row_spec(tm, D_MODEL),
        out_shape=jax.ShapeDtypeStruct((T, D_MODEL), F32),
        compiler_params=_params("parallel"),
    )(*args)


def _cumsum_kernel(x_ref, o_ref, *, n_blk):
    tril = _tril(PAGE).astype(F32)
    carry = jnp.zeros((1, N_HEADS), F32)
    for c in range(n_blk):
        y = _dot(tril, x_ref[0, c * PAGE:(c + 1) * PAGE, :], HIGHEST) + carry
        o_ref[0, c * PAGE:(c + 1) * PAGE, :] = y
        carry = y[PAGE - 1:PAGE, :]


def _cumsum_seq(logf):
    B, L, H = logf.shape
    return pl.pallas_call(
        functools.partial(_cumsum_kernel, n_blk=L // PAGE),
        grid=(B,),
        in_specs=[pl.BlockSpec((1, L, H), lambda b: (b, 0, 0))],
        out_specs=pl.BlockSpec((1, L, H), lambda b: (b, 0, 0)),
        out_shape=jax.ShapeDtypeStruct((B, L, H), F32),
        compiler_params=_params("parallel"),
    )(logf)


def _softmax_step(s, m, l, acc, pv_fn):
    m_new = jnp.maximum(m, jnp.max(s, axis=-1, keepdims=True))
    alpha = jnp.exp(m - m_new)
    p = jnp.exp(s - m_new)
    l = alpha * l + jnp.sum(p, axis=-1, keepdims=True)
    acc = alpha * acc + pv_fn(p.astype(BF16))
    return m_new, l, acc


def _fox_attn_kernel(q_ref, k_ref, v_ref, ck_ref, o_ref, *, tq, tk, ts):
    i = pl.program_id(2)
    n_sub = tq // ts
    qs = [q_ref[0, r * ts:(r + 1) * ts, :] for r in range(n_sub)]

    def step(j, carry, masked):
        start = pl.multiple_of(j * tk, tk)
        k = k_ref[0, pl.ds(start, tk), :]
        v = v_ref[0, pl.ds(start, tk), :]
        ck = ck_ref[0, 0, j]
        out = []
        for r in range(n_sub):
            s = _dot_nt(qs[r], k) - ck
            if masked:
                s = jnp.where(start + _iota((ts, tk), 1) <= i * tq + r * ts + _iota((ts, tk), 0), s, NEG)
            out.append(_softmax_step(s, *carry[r], lambda p: _dot(p, v)))
        return tuple(out)

    carry = tuple((jnp.full((ts, 1), NEG, F32), jnp.zeros((ts, 1), F32), jnp.zeros((ts, HEAD_DIM), F32))
                  for _ in range(n_sub))
    n_diag = tq // tk
    carry = lax.fori_loop(0, i * n_diag, lambda j, c: step(j, c, False), carry)
    for d in range(n_diag):
        carry = step(i * n_diag + d, carry, True)
    for r, (_, l, acc) in enumerate(carry):
        o_ref[0, r * ts:(r + 1) * ts, :] = (acc / l).astype(BF16)


def _fox_attn_prompt(q, k, v, cum, tq, tk, ts):
    B, L, _ = q.shape
    nb = L // tq
    ck = jnp.swapaxes(cum, 1, 2).reshape(B, N_HEADS, L // tk, 1, tk)
    return pl.pallas_call(
        functools.partial(_fox_attn_kernel, tq=tq, tk=tk, ts=ts),
        grid=(B, N_HEADS, nb),
        in_specs=[
            pl.BlockSpec((1, tq, HEAD_DIM), lambda b, h, i: (b, i, h)),
            pl.BlockSpec((1, L, HEAD_DIM), lambda b, h, i: (b, 0, h)),
            pl.BlockSpec((1, L, HEAD_DIM), lambda b, h, i: (b, 0, h)),
            pl.BlockSpec((1, 1, L // tk, 1, tk), lambda b, h, i: (b, h, 0, 0, 0)),
        ],
        out_specs=pl.BlockSpec((1, tq, HEAD_DIM), lambda b, h, i: (b, i, h)),
        out_shape=jax.ShapeDtypeStruct((B, L, BRANCH), BF16),
        compiler_params=_params("parallel", "parallel", "arbitrary"),
    )(q, k, v, ck)


def _cumsum_lanes(x):
    lane = _iota(x.shape, 1)
    d = 1
    while d < x.shape[1]:
        x = x + jnp.where(lane >= d, pltpu.roll(x, d, 1), 0.0)
        d *= 2
    return x


def _fox_dec_kernel(pt_ref, q_ref, kn_ref, vn_ref, lfn_ref, *rest, pps, n_steps, dl):
    k_refs, v_refs, lf_refs = rest[0:pps], rest[pps:2 * pps], rest[2 * pps:3 * pps]
    o_ref = rest[3 * pps]
    m_ref, l_ref, acc_ref, car_ref = rest[3 * pps + 1:]
    p = pl.program_id(1)
    nr = N_HEADS * dl
    n_chain = m_ref.shape[0]

    @pl.when(p == 0)
    def _():
        m_ref[...] = jnp.full(m_ref.shape, NEG, F32)
        l_ref[...] = jnp.zeros(l_ref.shape, F32)
        acc_ref[...] = jnp.zeros(acc_ref.shape, F32)
        car_ref[...] = jnp.zeros((N_HEADS, LANES), F32)

    qs = [q_ref[0, :, h * HEAD_DIM:(h + 1) * HEAD_DIM] for h in range(N_HEADS)]

    def attend(chains, n_tok, mask):
        off, cums = car_ref[...], {}
        for ci, (_, blocks) in enumerate(chains):
            for bi, (_, _, lf) in enumerate(blocks):
                c = _cumsum_lanes(lf)
                cums[ci, bi] = (c + off)[:, 0:n_tok]
                off = off + jnp.broadcast_to(c[:, LANES - 1:LANES], (N_HEADS, LANES))
        car_ref[...] = off

        def head_rows(blocks, which, h):
            return jnp.concatenate([blk[which][_head_rows(h, n_tok, N_HEADS), :].astype(BF16) for blk in blocks],
                                   axis=0)

        ss = []
        for ci, (_, blocks) in enumerate(chains):
            cum = jnp.concatenate([cums[ci, bi] for bi in range(len(blocks))], axis=1)
            s = jnp.concatenate([_dot_nt(qs[h], head_rows(blocks, 0, h)) - cum[h:h + 1, :] for h in range(N_HEADS)],
                                axis=0)
            ss.append(s if mask is None else jnp.where(mask, s, NEG))
        for (st, blocks), s in zip(chains, ss):
            def pv(pb, blocks=blocks):
                return jnp.concatenate([_dot(pb[h * dl:(h + 1) * dl], head_rows(blocks, 1, h))
                                        for h in range(N_HEADS)], axis=0)
            m_ref[st], l_ref[st], acc_ref[st] = _softmax_step(s, m_ref[st], l_ref[st], acc_ref[st], pv)

    per_chain = pps // n_chain
    attend([(ci, [(k_refs[r], v_refs[r], lf_refs[r][...]) for r in range(ci * per_chain, (ci + 1) * per_chain)])
            for ci in range(n_chain)], PAGE, None)

    @pl.when(p == n_steps - 1)
    def _():
        causal = _iota((nr, dl), 1) <= _iota((nr, dl), 0) % dl
        attend([(0, [(kn_ref, vn_ref, lfn_ref[0])])], dl, causal)
        m = m_ref[0]
        for ci in range(1, n_chain):
            m = jnp.maximum(m, m_ref[ci])
        l, acc = jnp.zeros((nr, 1), F32), jnp.zeros((nr, HEAD_DIM), F32)
        for ci in range(n_chain):
            w = jnp.exp(m_ref[ci] - m)
            l, acc = l + w * l_ref[ci], acc + w * acc_ref[ci]
        out = acc / l
        for h in range(N_HEADS):
            o_ref[0, :, h * HEAD_DIM:(h + 1) * HEAD_DIM] = out[h * dl:(h + 1) * dl].astype(BF16)


def _fox_attn_decode(q, kn, vn, lfn, cache_k, cache_v, cache_lf, layer, pt_flat, n_pages, pps):
    DB, DL, _ = q.shape
    n_steps = n_pages // pps
    nr = N_HEADS * DL
    n_chain = max(pps // 4, 1)
    lfn_t =jnp.pad(jnp.swapaxes(lfn, 1, 2), ((0, 0), (0, 0), (0, LANES - DL)))

    def page_spec(rows, r):
        return pl.BlockSpec((None, None, rows, LANES),
                            lambda b, p, pt, r=r: (layer, pt[b * n_pages + p * pps + r], 0, 0))

    new_rows = pl.BlockSpec((DL * N_HEADS, HEAD_DIM), lambda b, p, pt: (b, 0))
    tok = lambda n, w: pl.BlockSpec((1, n, w), lambda b, p, pt: (b, 0, 0))
    in_specs = [tok(DL, BRANCH), new_rows, new_rows, tok(N_HEADS, LANES)]
    in_specs += [page_spec(PAGE * N_HEADS, r) for r in range(pps)] * 2 + [page_spec(N_HEADS, r) for r in range(pps)]
    grid_spec = pltpu.PrefetchScalarGridSpec(
        num_scalar_prefetch=1, grid=(DB, n_steps), in_specs=in_specs, out_specs=tok(DL, BRANCH),
        scratch_shapes=[pltpu.VMEM((n_chain, nr, 1), F32), pltpu.VMEM((n_chain, nr, 1), F32),
                        pltpu.VMEM((n_chain, nr, HEAD_DIM), F32), pltpu.VMEM((N_HEADS, LANES), F32)])
    return pl.pallas_call(
        functools.partial(_fox_dec_kernel, pps=pps, n_steps=n_steps, dl=DL),
        grid_spec=grid_spec,
        out_shape=jax.ShapeDtypeStruct((DB, DL, BRANCH), BF16),
        compiler_params=_params("parallel", "arbitrary"),
    )(pt_flat, q, kn, vn, lfn_t, *([cache_k] * pps), *([cache_v] * pps), *([cache_lf] * pps))


def _key_to_float(t):
    return pltpu.bitcast(t ^ ((t >> 31) & 0x7FFFFFFF), F32)


def _kth_largest(count_ge, shape, n_sel, bits_per_step):
    def step(si, t):
        shift = 32 - bits_per_step * (si + 1)
        best = t
        for j in range(1, 2 ** bits_per_step):
            cand = t ^ lax.shift_left(jnp.int32(j), shift)
            best = jnp.where(count_ge(_key_to_float(cand)) >= n_sel, jnp.maximum(best, cand), best)
        return best
    t = lax.fori_loop(0, 32 // bits_per_step, step, jnp.full(shape, INT_MIN, jnp.int32))
    return _key_to_float(jnp.maximum(t, NEG_INF_KEY + 1))


def _strict_upper(n):
    return (_iota((n, n), 0) < _iota((n, n), 1)).astype(BF16)


def _dsa_attn_kernel(q_ref, qi_ref, wi_ref, k_ref, v_ref, ki_ref, o_ref, sc_ref, sct_ref, wb_ref, *, n_sel, tq, W):
    i = pl.program_id(1)
    nch = (i * tq + tq + W - 1) // W
    reps = W // LANES
    wi = wi_ref[0]
    for h in range(IDX_HEADS):
        wb_ref[h] = jnp.broadcast_to(wi[:, h:h + 1], (tq, LANES))
    row, col = _iota((tq, W), 0), _iota((tq, W), 1)

    def fill(c, _):
        kc = ki_ref[0, pl.ds(pl.multiple_of(c * W, W), W), :].astype(BF16)
        sc = jnp.zeros((tq, W), F32)
        for h in range(IDX_HEADS):
            d = _dot_nt(qi_ref[0, :, h * IDX_DIM:(h + 1) * IDX_DIM], kc)
            sc = sc + jnp.concatenate([wb_ref[h]] * reps, axis=1) * jnp.maximum(d, 0.0)
        sc = jnp.where(c * W + col <= i * tq + row, sc, -jnp.inf)
        sc_ref[c] = sc
        sct_ref[c] = sc.T
        return 0

    lax.fori_loop(0, nch, fill, 0)

    def count(pred):
        def body(c, acc):
            x = jnp.where(pred(sc_ref[c]), 1.0, 0.0)
            for r in range(reps):
                acc = acc + x[:, r * LANES:(r + 1) * LANES]
            return acc
        return jnp.sum(lax.fori_loop(0, nch, body, jnp.zeros((tq, LANES), F32)), axis=-1, keepdims=True)

    def count_ge_t(f):
        f8 = jnp.broadcast_to(f, (SUBLANES, tq))
        n_acc = 4

        def body(c, accs):
            accs = list(accs)
            for r in range(W // SUBLANES):
                x = jnp.where(sct_ref[c, r * SUBLANES:(r + 1) * SUBLANES, :] >= f8, 1.0, 0.0)
                accs[r % n_acc] = accs[r % n_acc] + x
            return tuple(accs)

        accs = lax.fori_loop(0, nch, body, tuple(jnp.zeros((SUBLANES, tq), F32) for _ in range(n_acc)))
        return jnp.sum((accs[0] + accs[1]) + (accs[2] + accs[3]), axis=0, keepdims=True)

    thr_row = _kth_largest(count_ge_t, (1, tq), n_sel, 1)
    eye = _iota((tq, tq), 0) == _iota((tq, tq), 1)
    thr = jnp.sum(jnp.where(eye, jnp.broadcast_to(thr_row, (tq, tq)), 0.0), axis=-1, keepdims=True)

    @pl.when(jnp.max(count(lambda s: s >= thr)) > n_sel)
    def _():
        need = n_sel - count(lambda s: s > thr)
        upper = _strict_upper(W)

        def body(c, seen):
            sc = sc_ref[c]
            eq = sc == thr
            eqf = jnp.where(eq, 1.0, 0.0)
            rank = seen + _dot(eqf.astype(BF16), upper)
            sc_ref[c] = jnp.where(eq & (rank >= need), -jnp.inf, sc)
            return seen + jnp.sum(eqf, axis=-1, keepdims=True)

        lax.fori_loop(0, nch, body, jnp.zeros((tq, 1), F32))

    qgs = [jnp.concatenate(
        [q_ref[0, :, (g * DSA_GROUP + r) * HEAD_DIM:(g * DSA_GROUP + r + 1) * HEAD_DIM] for r in range(DSA_GROUP)],
        axis=0) for g in range(DSA_KV_HEADS)]

    def body(c, carry):
        start = pl.multiple_of(c * W, W)
        bias = jnp.where(sc_ref[c] >= thr, 0.0, NEG)[None]
        out = []
        for g in range(DSA_KV_HEADS):
            m, l, acc = carry[g]
            gsl = slice(g * HEAD_DIM, (g + 1) * HEAD_DIM)
            s = _dot_nt(qgs[g], k_ref[0, pl.ds(start, W), gsl]).reshape(DSA_GROUP, tq, W) + bias
            m_new = jnp.maximum(m, jnp.max(s, axis=-1, keepdims=True))
            alpha = jnp.exp(m - m_new)
            p = jnp.exp(s - m_new)
            l = alpha * l + jnp.sum(p, axis=-1, keepdims=True)
            pv = _dot(p.reshape(DSA_GROUP * tq, W).astype(BF16), v_ref[0, pl.ds(start, W), gsl])
            out.append((m_new, l, alpha * acc + pv.reshape(DSA_GROUP, tq, HEAD_DIM)))
        return tuple(out)

    init = tuple((jnp.full((DSA_GROUP, tq, 1), NEG, F32), jnp.zeros((DSA_GROUP, tq, 1), F32),
                  jnp.zeros((DSA_GROUP, tq, HEAD_DIM), F32)) for _ in range(DSA_KV_HEADS))
    res = lax.fori_loop(0, nch, body, init)
    for g, (_, l, acc) in enumerate(res):
        out = acc / l
        for r in range(DSA_GROUP):
            hh = g * DSA_GROUP + r
            o_ref[0, :, hh * HEAD_DIM:(hh + 1) * HEAD_DIM] = out[r].astype(BF16)


def _dsa_attn_prompt(q, qi, wi, k, v, ki, n_sel, tq):
    B, L, _ = q.shape
    nb = L // tq
    W = min(4 * LANES, L)
    blk = lambda w: pl.BlockSpec((1, tq, w), lambda b, i: (b, i, 0))
    full = lambda w: pl.BlockSpec((1, L, w), lambda b, i: (b, 0, 0))
    return pl.pallas_call(
        functools.partial(_dsa_attn_kernel, n_sel=n_sel, tq=tq, W=W),
        grid=(B, nb),
        in_specs=[blk(BRANCH), blk(IDX_HEADS * IDX_DIM), blk(IDX_HEADS), full(DSA_KV), full(DSA_KV), full(IDX_DIM)],
        out_specs=blk(BRANCH),
        out_shape=jax.ShapeDtypeStruct((B, L, BRANCH), BF16),
        scratch_shapes=[pltpu.VMEM((L // W, tq, W), F32), pltpu.VMEM((L // W, W, tq), F32),
                        pltpu.VMEM((IDX_HEADS, tq, LANES), F32)],
        compiler_params=_params("parallel", "arbitrary"),
    )(q, qi, wi, k, v, ki)


def _dsa_dec_kernel(pt_ref, q_ref, qi_ref, wi_ref, kn_ref, vn_ref, kin_ref, *rest, pps, n_pages, dl, n_sel, bb):
    n_pg = bb * pps
    k_refs, v_refs, ki_refs = rest[0:n_pg], rest[n_pg:2 * n_pg], rest[2 * n_pg:3 * n_pg]
    o_ref = rest[3 * n_pg]
    ks_ref, vs_ref, kis_ref = rest[3 * n_pg + 1:]
    p = pl.program_id(1)
    past = n_pages * PAGE
    nk = past + PAGE
    nr = bb * dl
    G = DSA_KV_HEADS

    for e in range(bb):
        for r in range(pps):
            page = p * pps + r
            start = pl.multiple_of(page * (PAGE * G), PAGE * G)
            ks_ref[e, pl.ds(start, PAGE * G), :] = k_refs[e * pps + r][...]
            vs_ref[e, pl.ds(start, PAGE * G), :] = v_refs[e * pps + r][...]
            kis_ref[e, page] = ki_refs[e * pps + r][...]

    @pl.when(p == n_pages // pps - 1)
    def _():
        for e in range(bb):
            for ref, new in ((ks_ref, kn_ref), (vs_ref, vn_ref)):
                ref[e, past * G:(past + dl) * G, :] = new[e * dl * G:(e + 1) * dl * G, :]
                ref[e, (past + dl) * G:nk * G, :] = jnp.zeros(((PAGE - dl) * G, HEAD_DIM), F32)
            kis_ref[e, n_pages] = kin_ref[e]

        rows = []
        for e in range(bb):
            qi, wi = qi_ref[e], wi_ref[e]
            chunks = []
            for c in range(n_pages + 1):
                wr = wi * jnp.maximum(_dot(qi, kis_ref[e, c].astype(BF16)), 0.0)
                sc = wr[0:dl]
                for h in range(1, IDX_HEADS):
                    sc = sc + wr[h * dl:(h + 1) * dl]
                chunks.append(sc)
            rows.append(jnp.concatenate(chunks, axis=1))
        score = jnp.concatenate(rows, axis=0)
        valid = _iota((nr, nk), 1) <= past + _iota((nr, nk), 0) % dl
        score = jnp.where(valid, score, -jnp.inf)

        def count(mask):
            return jnp.sum(jnp.where(mask, 1.0, 0.0), axis=-1, keepdims=True)

        thr = _kth_largest(lambda f: count(score >= f), (nr, 1), n_sel, 2)

        def drop_surplus_ties(sc):
            need = n_sel - count(sc > thr)
            upper = _strict_upper(PAGE)
            seen, kept = jnp.zeros((nr, 1), F32), []
            for c in range(n_pages + 1):
                scc = sc[:, c * PAGE:(c + 1) * PAGE]
                eq = scc == thr
                eqf = jnp.where(eq, 1.0, 0.0)
                rank = seen + _dot(eqf.astype(BF16), upper)
                kept.append(jnp.where(eq & (rank >= need), -jnp.inf, scc))
                seen = seen + jnp.sum(eqf, axis=-1, keepdims=True)
            return jnp.concatenate(kept, axis=1)

        score = lax.cond(jnp.max(count(score >= thr)) > n_sel, drop_surplus_ties, lambda sc: sc, score)
        sel = score >= thr

        units = [(e, g) for e in range(bb) for g in range(G)]

        def stage(fn):
            return [fn(i, e, g) for i, (e, g) in enumerate(units)]

        qg = stage(lambda i, e, g: jnp.concatenate(
            [q_ref[e, :, (g * DSA_GROUP + r) * HEAD_DIM:(g * DSA_GROUP + r + 1) * HEAD_DIM].astype(F32)
             for r in range(DSA_GROUP)], axis=0).astype(BF16))
        s = stage(lambda i, e, g: jnp.where(
            sel[e * dl:(e + 1) * dl][None],
            _dot_nt(qg[i], ks_ref[e, _head_rows(g, nk, G), :].astype(BF16)).reshape(DSA_GROUP, dl, nk), NEG))
        pr = stage(lambda i, e, g: jnp.exp(s[i] - jnp.max(s[i], axis=-1, keepdims=True)))
        den = stage(lambda i, e, g: jnp.sum(pr[i], axis=-1, keepdims=True))
        pv = stage(lambda i, e, g: _dot(pr[i].reshape(DSA_GROUP * dl, nk).astype(BF16),
                                        vs_ref[e, _head_rows(g, nk, G), :].astype(BF16)))
        for i, (e, g) in enumerate(units):
            out = pv[i].reshape(DSA_GROUP, dl, HEAD_DIM) / den[i]
            for r in range(DSA_GROUP):
                hh = g * DSA_GROUP + r
                o_ref[e, :, hh * HEAD_DIM:(hh + 1) * HEAD_DIM] = out[r].astype(BF16)


def _dsa_attn_decode(q, qi, wi, kn, vn, kin, cache_k, cache_v, cache_ki, layer, pt_flat, n_pages, pps, n_sel):
    DB, DL, _ = q.shape
    nk = n_pages * PAGE + PAGE
    G = DSA_KV_HEADS
    bb = max(d for d in (1, 2, 4) if DB % d == 0)
    qi_hq = jnp.swapaxes(qi.reshape(DB, DL, IDX_HEADS, IDX_DIM), 1, 2).reshape(DB, IDX_HEADS * DL, IDX_DIM)
    wi_hq = jnp.swapaxes(wi.reshape(DB, DL, IDX_HEADS), 1, 2).reshape(DB, IDX_HEADS * DL, 1)
    kin_t = jnp.pad(jnp.swapaxes(kin, 1, 2), ((0, 0), (0, 0), (0, PAGE - DL)))

    def page_specs(rows):
        return [pl.BlockSpec((None, None, rows, LANES),
                             lambda b, p, pt, e=e, r=r: (layer, pt[(b * bb + e) * n_pages + p * pps + r], 0, 0))
                for e in range(bb) for r in range(pps)]

    tok = lambda n, w: pl.BlockSpec((bb, n, w), lambda b, p, pt: (b, 0, 0))
    new_rows = pl.BlockSpec((bb * DL * G, HEAD_DIM), lambda b, p, pt: (b, 0))
    in_specs = [tok(DL, BRANCH), tok(IDX_HEADS * DL, IDX_DIM), tok(IDX_HEADS * DL, 1),
                new_rows, new_rows, tok(IDX_DIM, PAGE)]
    in_specs += page_specs(PAGE * G) * 2 + page_specs(IDX_DIM)
    n_pg = bb * pps
    grid_spec = pltpu.PrefetchScalarGridSpec(
        num_scalar_prefetch=1, grid=(DB // bb, n_pages // pps), in_specs=in_specs, out_specs=tok(DL, BRANCH),
        scratch_shapes=[pltpu.VMEM((bb, nk * G, HEAD_DIM), F32), pltpu.VMEM((bb, nk * G, HEAD_DIM), F32),
                        pltpu.VMEM((bb, n_pages + 1, IDX_DIM, PAGE), F32)])
    return pl.pallas_call(
        functools.partial(_dsa_dec_kernel, pps=pps, n_pages=n_pages, dl=DL, n_sel=n_sel, bb=bb),
        grid_spec=grid_spec,
        out_shape=jax.ShapeDtypeStruct((DB, DL, BRANCH), BF16),
        compiler_params=_params("parallel", "arbitrary"),
    )(pt_flat, q, qi_hq, wi_hq, kn, vn, kin_t, *([cache_k] * n_pg), *([cache_v] * n_pg), *([cache_ki] * n_pg))


def _gdn_kernel(x_ref, z_ref, g_ref, b_ref, cw_ref, nw_ref, cs_ref, s0_ref, o_ref, sout_ref, xb_ref, s_ref,
                *, C, nc, bb):
    c = pl.program_id(1)
    HIST = SUBLANES

    @pl.when(c == 0)
    def _():
        xb_ref[:, 0:HIST, :] = cs_ref[...]
        s_ref[...] = s0_ref[...]

    incl, strict = _tril(C), _tril(C, strict=True)
    eye_c = (_iota((C, C), 0) == _iota((C, C), 1)).astype(F32)
    eye_h = (_iota((N_HEADS, N_HEADS), 0) == _iota((N_HEADS, N_HEADS), 1)).astype(F32)
    n_dbl = int(math.log2(C)) - 1

    convs, G_alls, Gt_alls = [], [], []
    for e in range(bb):
        xb_ref[e, HIST:HIST + C, :] = x_ref[e]
        conv = xb_ref[e, HIST:HIST + C, :] * cw_ref[CONV_W - 1:CONV_W, :]
        for j in range(CONV_W - 1):
            off = HIST - (CONV_W - 1) + j
            conv = conv + xb_ref[e, off:off + C, :] * cw_ref[j:j + 1, :]
        hist = xb_ref[e, C:C + HIST, :]
        xb_ref[e, 0:HIST, :] = hist
        convs.append(_silu(conv))
        G = _dot(incl.astype(F32), g_ref[e], HIGHEST)
        G_alls.append(G)
        Gt_alls.append(_dot_nt(eye_h, G, HIGHEST))

    units = [(e, h) for e in range(bb) for h in range(N_HEADS)]

    def stage(fn):
        return [fn(i, e, h) for i, (e, h) in enumerate(units)]

    def head(e, j, h):
        return convs[e][:, j * BRANCH + h * HEAD_DIM:j * BRANCH + (h + 1) * HEAD_DIM]

    def l2n(a):
        return a * lax.rsqrt(jnp.sum(a * a, axis=-1, keepdims=True) + EPS)

    q = stage(lambda i, e, h: l2n(head(e, 0, h)) * ATTN_SCALE)
    k = stage(lambda i, e, h: l2n(head(e, 1, h)))
    v = stage(lambda i, e, h: head(e, 2, h))
    Gc = stage(lambda i, e, h: G_alls[e][:, h:h + 1])
    bc = stage(lambda i, e, h: b_ref[e][:, h:h + 1])
    decay = stage(lambda i, e, h: jnp.exp(jnp.where(incl, Gc[i] - Gt_alls[e][h:h + 1, :], -jnp.inf)))
    kk = stage(lambda i, e, h: _dot1(k[i], k[i], _dot_nt))
    A = stage(lambda i, e, h: jnp.where(strict, bc[i] * kk[i] * decay[i], 0.0))
    T = stage(lambda i, e, h: eye_c - A[i])
    P = stage(lambda i, e, h: _dot1(A[i], A[i]))
    for it in range(n_dbl):
        TP = stage(lambda i, e, h: _dot1(T[i], P[i]))
        T = stage(lambda i, e, h: T[i] + TP[i])
        if it + 1 < n_dbl:
            P = stage(lambda i, e, h: _dot1(P[i], P[i]))
    eG = stage(lambda i, e, h: jnp.exp(Gc[i]))
    WU = stage(lambda i, e, h: _dot1(T[i], jnp.concatenate([k[i] * (bc[i] * eG[i]), v[i] * bc[i]], axis=1)))
    Aqk = stage(lambda i, e, h: _dot1(q[i], k[i], _dot_nt) * decay[i])
    G_last = stage(lambda i, e, h: Gc[i][C - 1:C, :])
    k_dec = stage(lambda i, e, h: k[i] * jnp.exp(G_last[i] - Gc[i]))
    S = stage(lambda i, e, h: s_ref[e, h])
    WqS = stage(lambda i, e, h: _dot1(jnp.concatenate([WU[i][:, 0:HEAD_DIM], q[i]], axis=0), S[i]))
    Vn = stage(lambda i, e, h: WU[i][:, HEAD_DIM:2 * HEAD_DIM] - WqS[i][0:C])
    o = stage(lambda i, e, h: eG[i] * WqS[i][C:2 * C] + _dot1(Aqk[i], Vn[i]))
    S_new = stage(lambda i, e, h: jnp.exp(G_last[i]) * S[i] + _dot1(k_dec[i], Vn[i], _dot_tn))
    for i, (e, h) in enumerate(units):
        sl = slice(h * HEAD_DIM, (h + 1) * HEAD_DIM)
        s_ref[e, h] = S_new[i]
        on = o[i] * lax.rsqrt(jnp.mean(o[i] * o[i], axis=-1, keepdims=True) + EPS) * nw_ref[...]
        o_ref[e, :, sl] = (on * _silu(z_ref[e, :, sl].astype(F32))).astype(BF16)

    @pl.when(c == nc - 1)
    def _():
        sout_ref[...] = s_ref[...]


def _gdn_mix(qkv, z, g, beta, conv_w, norm_w, conv_state, S0):
    B, L, _ = qkv.shape
    C = min(GDN_CHUNK, L)
    nc = L // C
    assert C & (C - 1) == 0 and L % C == 0
    bb = max(d for d in (1, 2, 4) if B % d == 0 and d * C <= GDN_CHUNK)
    cs = jnp.pad(conv_state, ((0, 0), (SUBLANES - (CONV_W - 1), 0), (0, 0)))
    blk = lambda w: pl.BlockSpec((bb, C, w), lambda b, c: (b, c, 0))
    state = pl.BlockSpec((bb, N_HEADS, HEAD_DIM, HEAD_DIM), lambda b, c: (b, 0, 0, 0))
    return pl.pallas_call(
        functools.partial(_gdn_kernel, C=C, nc=nc, bb=bb),
        grid=(B // bb, nc),
        in_specs=[blk(3 * BRANCH), blk(BRANCH), blk(N_HEADS), blk(N_HEADS),
                  pl.BlockSpec((CONV_W, 3 * BRANCH), lambda b, c: (0, 0)),
                  pl.BlockSpec((1, HEAD_DIM), lambda b, c: (0, 0)),
                  pl.BlockSpec((bb, SUBLANES, 3 * BRANCH), lambda b, c: (b, 0, 0)),
                  state],
        out_specs=[blk(BRANCH), state],
        out_shape=[jax.ShapeDtypeStruct((B, L, BRANCH), BF16),
                   jax.ShapeDtypeStruct((B, N_HEADS, HEAD_DIM, HEAD_DIM), F32)],
        scratch_shapes=[pltpu.VMEM((bb, C + SUBLANES, 3 * BRANCH), F32),
                        pltpu.VMEM((bb, N_HEADS, HEAD_DIM, HEAD_DIM), F32)],
        compiler_params=_params("parallel", "arbitrary"),
    )(qkv, z, g, beta, conv_w, norm_w.reshape(1, HEAD_DIM), cs, S0)


def _pad_cols(w, n=LANES):
    return jnp.pad(w, ((0, 0), (0, n - w.shape[1])))


def _row_tile(T):
    return 256 if T % 256 == 0 else T


def kernel(x_prompt, x_sample, cache_fox_k, cache_fox_v, cache_fox_logf, cache_dsa_k, cache_dsa_v, cache_dsa_kidx, state_gdn_conv, state_gdn_S, page_table, norm_w, norm_f, fox_w_in, fox_b_f, fox_w_out, dsa_w_in, dsa_w_out, gdn_w_in, gdn_conv_w, gdn_A_log, gdn_dt_bias, gdn_norm_w, gdn_w_out):
    B, L, _ = x_prompt.shape
    DB, DL, _ = x_sample.shape
    n_pages = page_table.shape[1]
    past = n_pages * PAGE
    depth = norm_w.shape[0]
    pt_flat = page_table.reshape(-1)
    pages_per_step = lambda want: max(p for p in (1, 2, 4, 8, 16) if p <= want and n_pages % p == 0)
    fox_pps, dsa_pps = pages_per_step(16), pages_per_step(4)
    groups = ((B, L), (DB, DL))
    xs = [x_prompt.reshape(B * L, D_MODEL), x_sample.reshape(DB * DL, D_MODEL)]
    tms = [_row_tile(nb * nl) for nb, nl in groups]
    pos = [jnp.arange(L), past + jnp.arange(DL)]
    tq_p = max(t for t in (PAGE, 2 * PAGE, 4 * PAGE) if L % t == 0)

    pool = cache_fox_k.shape[1]
    fox_ck = cache_fox_k.reshape(-1, pool, PAGE * N_HEADS, HEAD_DIM)
    fox_cv = cache_fox_v.reshape(-1, pool, PAGE * N_HEADS, HEAD_DIM)
    fox_clf = jnp.swapaxes(cache_fox_logf, 2, 3)
    dsa_ck = cache_dsa_k.reshape(-1, pool, PAGE * DSA_KV_HEADS, HEAD_DIM)
    dsa_cv = cache_dsa_v.reshape(-1, pool, PAGE * DSA_KV_HEADS, HEAD_DIM)
    dsa_cki = jnp.swapaxes(cache_dsa_kidx, 2, 3)

    st = {name: ([], []) for name in ("fox_k", "fox_v", "fox_lf", "dsa_k", "dsa_v", "dsa_ki", "gdn_conv", "gdn_S")}
    finals = [None, None]
    n_fox = fox_w_in.shape[0]
    fox_kv_stack = [None, None]

    for i in range(depth):
        kind, j = i % 3, i // 3
        nw = norm_w[i].reshape(1, D_MODEL)
        last = i == depth - 1
        for gi, (nb, nl) in enumerate(groups):
            x2d, tm = xs[gi], tms[gi]
            T = nb * nl
            prompt = gi == 0
            sh = lambda a: a.reshape(nb, nl, a.shape[-1])
            if kind == 0:
                w = fox_w_in[j]
                w_main = jnp.concatenate([w[:, :3 * BRANCH], w[:, 3 * BRANCH + N_HEADS:]], axis=1).astype(BF16)
                w_small = _pad_cols(w[:, 3 * BRANCH:3 * BRANCH + N_HEADS]).astype(BF16)
                outs = [(1, BRANCH, BF16), (N_HEADS, HEAD_DIM, F32), (N_HEADS, HEAD_DIM, F32), (1, BRANCH, BF16),
                        (1, N_HEADS, F32)]
                stacked = None
                if prompt:
                    outs += [(1, BRANCH, BF16), (1, BRANCH, BF16)]
                    stacked = (j, n_fox, {1: fox_kv_stack[0], 2: fox_kv_stack[1]})
                res = _in_proj(_fox_in_kernel, x2d, [nw, w_main, w_small, fox_b_f[j].reshape(1, N_HEADS)], [], outs,
                               tm, stacked)
                q, k, v, g, lf = res[:5]
                if prompt:
                    o = _fox_attn_prompt(sh(q), sh(res[5]), sh(res[6]), _cumsum_seq(sh(lf)), tq_p, tq_p, tq_p)
                    fox_kv_stack = [k, v]
                else:
                    o = _fox_attn_decode(sh(q), k, v, sh(lf), fox_ck, fox_cv, fox_clf, j, pt_flat, n_pages, fox_pps)
                    st["fox_k"][gi].append(k.reshape(nb, nl, N_HEADS, HEAD_DIM))
                    st["fox_v"][gi].append(v.reshape(nb, nl, N_HEADS, HEAD_DIM))
                st["fox_lf"][gi].append(lf.reshape(nb, nl, N_HEADS))
                w_out = fox_w_out[j]
            elif kind == 1:
                w = dsa_w_in[j]
                o_qi = BRANCH + 2 * DSA_KV
                o_wi = o_qi + IDX_HEADS * IDX_DIM
                o_ki = o_wi + IDX_HEADS
                o_g = o_ki + IDX_DIM
                w_main = jnp.concatenate([w[:, :o_wi], w[:, o_g:]], axis=1).astype(BF16)
                w_small = _pad_cols(jnp.concatenate([w[:, o_ki:o_g], w[:, o_wi:o_ki]], axis=1)).astype(BF16)
                tables = list(_rope_tables(pos[gi], HEAD_DIM, tm) + _rope_tables(pos[gi], IDX_DIM, tm))
                outs = [(1, BRANCH, BF16), (DSA_KV_HEADS, HEAD_DIM, F32), (DSA_KV_HEADS, HEAD_DIM, F32),
                        (1, IDX_HEADS * IDX_DIM, BF16), (1, BRANCH, BF16), (1, IDX_DIM, F32), (1, IDX_HEADS, F32)]
                if prompt:
                    outs += [(1, DSA_KV, BF16), (1, DSA_KV, BF16)]
                res = _in_proj(_dsa_in_kernel, x2d, [nw, w_main, w_small], tables, outs, tm)
                q, k, v, qi, g, ki, wi = res[:7]
                if prompt:
                    o = _dsa_attn_prompt(sh(q), sh(qi), sh(wi), sh(res[7]), sh(res[8]), sh(ki),
                                         min(TOPK_MAX, L // 4), PAGE)
                else:
                    o = _dsa_attn_decode(sh(q), sh(qi), sh(wi), k, v, sh(ki), dsa_ck, dsa_cv, dsa_cki,
                                         j, pt_flat, n_pages, dsa_pps, min(TOPK_MAX, (past + DL) // 4))
                st["dsa_k"][gi].append(k.reshape(nb, nl, DSA_KV_HEADS, HEAD_DIM))
                st["dsa_v"][gi].append(v.reshape(nb, nl, DSA_KV_HEADS, HEAD_DIM))
                st["dsa_ki"][gi].append(ki.reshape(nb, nl, IDX_DIM))
                w_out = dsa_w_out[j]
            else:
                w = gdn_w_in[j]
                w_main = jnp.concatenate([w[:, :3 * BRANCH], w[:, 3 * BRANCH + 2 * N_HEADS:]], axis=1).astype(BF16)
                w_small = _pad_cols(w[:, 3 * BRANCH:3 * BRANCH + 2 * N_HEADS]).astype(BF16)
                qkv, z, gg, beta = _in_proj(
                    _gdn_in_kernel, x2d,
                    [nw, w_main, w_small, gdn_A_log[j].reshape(1, N_HEADS), gdn_dt_bias[j].reshape(1, N_HEADS)], [],
                    [(1, 3 * BRANCH, F32), (1, BRANCH, BF16), (1, N_HEADS, F32), (1, N_HEADS, F32)], tm)
                if prompt:
                    conv0 = jnp.zeros((nb, CONV_W - 1, 3 * BRANCH), F32)
                    S0 = jnp.zeros((nb, N_HEADS, HEAD_DIM, HEAD_DIM), F32)
                else:
                    conv0, S0 = state_gdn_conv[j], state_gdn_S[j]
                o, S_new = _gdn_mix(sh(qkv), sh(z), sh(gg), sh(beta), gdn_conv_w[j], gdn_norm_w[j], conv0, S0)
                tail = jnp.concatenate([conv0, sh(qkv)[:, -min(nl, CONV_W - 1):]], axis=1)
                st["gdn_conv"][gi].append(tail[:, -(CONV_W - 1):])
                st["gdn_S"][gi].append(S_new)
                g = None
                w_out = gdn_w_out[j]
            y = _out_proj(o.reshape(T, BRANCH), g, x2d, w_out.astype(BF16), norm_f if last else None, tm)
            if last:
                finals[gi] = y.reshape(nb, nl, D_MODEL)
            else:
                xs[gi] = y

    stk = lambda name, gi: jnp.stack(st[name][gi], axis=0)
    fox_kv_p = [a.reshape(n_fox, B, L, N_HEADS, HEAD_DIM) for a in fox_kv_stack]
    return (finals[0], finals[1],
            fox_kv_p[0], fox_kv_p[1], stk("fox_lf", 0),
            stk("fox_k", 1), stk("fox_v", 1), stk("fox_lf", 1),
            stk("dsa_k", 0), stk("dsa_v", 0), stk("dsa_ki", 0),
            stk("dsa_k", 1), stk("dsa_v", 1), stk("dsa_ki", 1),
            stk("gdn_conv", 0), stk("gdn_S", 0),
            stk("gdn_conv", 1), stk("gdn_S", 1))
```

```python
import functools
import math

import jax
import jax.numpy as jnp
from jax import lax
from jax.experimental import pallas as pl
from jax.experimental.pallas import tpu as pltpu

F32 = jnp.float32
BF16 = jnp.bfloat16
HIGHEST = lax.Precision.HIGHEST

D_MODEL = 1024
N_HEADS = 8
HEAD_DIM = 128
BRANCH = N_HEADS * HEAD_DIM
ATTN_SCALE = HEAD_DIM ** -0.5
DSA_KV_HEADS = 2
DSA_GROUP = N_HEADS // DSA_KV_HEADS
DSA_KV = DSA_KV_HEADS * HEAD_DIM
IDX_HEADS = 8
IDX_DIM = 64
IDX_SCALE = IDX_DIM ** -0.5
TOPK_MAX = 256
CONV_W = 4
GDN_CHUNK = 64
ROPE_THETA = 10000.0
EPS = 1e-6
PAGE = 128
LANES = 128
SUBLANES = 8
NEG = -1e30
INT_MIN = -(2 ** 31)
NEG_INF_KEY = INT_MIN + 0x007FFFFF
V7X_VMEM_BYTES = 64 * 1024 * 1024
VMEM_LIMIT = V7X_VMEM_BYTES * 7 // 8


def _dot(a, b, prec=None):
    return jnp.dot(a, b, preferred_element_type=F32, precision=prec)


def _dot_nt(a, b, prec=None):
    return lax.dot_general(a, b, (((1,), (1,)), ((), ())), preferred_element_type=F32, precision=prec)


def _dot_tn(a, b, prec=None):
    return lax.dot_general(a, b, (((0,), (0,)), ((), ())), preferred_element_type=F32, precision=prec)


def _dot1(a, b, dot=_dot):
    return dot(a.astype(BF16), b.astype(BF16))


def _iota(shape, axis):
    return lax.broadcasted_iota(jnp.int32, shape, axis)


def _tril(n, strict=False):
    r, c = _iota((n, n), 0), _iota((n, n), 1)
    return (r > c) if strict else (r >= c)


def _sigmoid(x):
    return 1.0 / (1.0 + jnp.exp(-x))


def _silu(x):
    return x * _sigmoid(x)


def _softplus(x):
    return jnp.maximum(x, 0.0) + jnp.log1p(jnp.exp(-jnp.abs(x)))


def _params(*sem):
    return pltpu.CompilerParams(dimension_semantics=sem, vmem_limit_bytes=VMEM_LIMIT)


def _head_rows(h, n_tok, n_heads):
    return pl.ds(h, n_tok, stride=n_heads)


def _normed(x_ref, nw_ref):
    x = x_ref[...]
    ms = jnp.mean(x * x, axis=-1, keepdims=True)
    return (x * lax.rsqrt(ms + EPS) * nw_ref[...]).astype(BF16)


def _fox_in_kernel(x_ref, nw_ref, w_ref, ws_ref, bf_ref, q_ref, k_ref, v_ref, g_ref, lf_ref, *copies):
    tm = x_ref.shape[0]
    h = _normed(x_ref, nw_ref)
    q_ref[...] = (_dot(h, w_ref[:, 0:BRANCH]) * ATTN_SCALE).astype(BF16)
    for j, ref in ((1, k_ref), (2, v_ref)):
        r = _dot(h, w_ref[:, j * BRANCH:(j + 1) * BRANCH])
        for hh in range(N_HEADS):
            ref[_head_rows(hh, tm, N_HEADS), :] = r[:, hh * HEAD_DIM:(hh + 1) * HEAD_DIM]
        if copies:
            copies[j - 1][...] = r.astype(BF16)
    g_ref[...] = _dot(h, w_ref[:, 3 * BRANCH:4 * BRANCH]).astype(BF16)
    f = _dot(h, ws_ref[...])[:, 0:N_HEADS] + bf_ref[...]
    lf_ref[...] = -_softplus(-f)


def _rope128(x, cos, sin_signed):
    return x * cos + pltpu.roll(x, HEAD_DIM // 2, 1) * sin_signed


def _rope64(x, cos, sin_signed):
    lane = _iota(x.shape, 1)
    rot = jnp.where((lane & (IDX_DIM - 1)) < IDX_DIM // 2,
                    pltpu.roll(x, LANES - IDX_DIM // 2, 1), pltpu.roll(x, IDX_DIM // 2, 1))
    return x * cos + rot * sin_signed


def _dsa_in_kernel(x_ref, nw_ref, w_ref, ws_ref, c128_ref, s128_ref, c64_ref, s64_ref,
                   q_ref, k_ref, v_ref, qi_ref, g_ref, ki_ref, wi_ref, *copies):
    tm = x_ref.shape[0]
    h = _normed(x_ref, nw_ref)
    c128, s128, c64, s64 = c128_ref[...], s128_ref[...], c64_ref[...], s64_ref[...]
    o_k, o_v, o_qi, o_g = BRANCH, BRANCH + DSA_KV, BRANCH + 2 * DSA_KV, BRANCH + 2 * DSA_KV + IDX_HEADS * IDX_DIM
    q_all = _dot(h, w_ref[:, 0:BRANCH])
    for hh in range(N_HEADS):
        sl = slice(hh * HEAD_DIM, (hh + 1) * HEAD_DIM)
        q_ref[:, sl] = (_rope128(q_all[:, sl], c128, s128) * ATTN_SCALE).astype(BF16)
    k_all = _dot(h, w_ref[:, o_k:o_k + DSA_KV])
    v_all = _dot(h, w_ref[:, o_v:o_v + DSA_KV])
    for hh in range(DSA_KV_HEADS):
        sl = slice(hh * HEAD_DIM, (hh + 1) * HEAD_DIM)
        kk = _rope128(k_all[:, sl], c128, s128)
        k_ref[_head_rows(hh, tm, DSA_KV_HEADS), :] = kk
        v_ref[_head_rows(hh, tm, DSA_KV_HEADS), :] = v_all[:, sl]
        if copies:
            copies[0][:, sl] = kk.astype(BF16)
    if copies:
        copies[1][...] = v_all.astype(BF16)
    qi_all = _dot(h, w_ref[:, o_qi:o_qi + IDX_HEADS * IDX_DIM])
    for hh in range(IDX_HEADS * IDX_DIM // LANES):
        sl = slice(hh * LANES, (hh + 1) * LANES)
        qi_ref[:, sl] = _rope64(qi_all[:, sl], c64, s64).astype(BF16)
    g_ref[...] = _dot(h, w_ref[:, o_g:o_g + BRANCH]).astype(BF16)
    small = _dot(h, ws_ref[...])
    ki_ref[...] = _rope64(small, c64, s64)[:, 0:IDX_DIM]
    wi_ref[...] = small[:, IDX_DIM:IDX_DIM + IDX_HEADS] * (IDX_HEADS ** -0.5 * IDX_SCALE)


def _gdn_in_kernel(x_ref, nw_ref, w_ref, ws_ref, alog_ref, dtb_ref, qkv_ref, z_ref, g_ref, beta_ref):
    h = _normed(x_ref, nw_ref)
    for j in range(3):
        sl = slice(j * BRANCH, (j + 1) * BRANCH)
        qkv_ref[:, sl] = _dot(h, w_ref[:, sl])
    z_ref[...] = _dot(h, w_ref[:, 3 * BRANCH:4 * BRANCH]).astype(BF16)
    small = _dot(h, ws_ref[...])
    a = small[:, 0:N_HEADS]
    b = small[:, N_HEADS:2 * N_HEADS]
    g_ref[...] = -jnp.exp(alog_ref[...]) * _softplus(a + dtb_ref[...])
    beta_ref[...] = _sigmoid(b)


def _row_spec(tm, n):
    return pl.BlockSpec((tm, n), lambda i: (i, 0))


def _full_spec(shape):
    return pl.BlockSpec(shape, lambda i: (0,) * len(shape))


def _in_proj(kern, x2d, consts, tables, outs, tm, stacked=None):
    T = x2d.shape[0]
    in_specs = [_row_spec(tm, D_MODEL)] + [_full_spec(c.shape) for c in consts[:3]]
    args = [x2d] + list(consts[:3])
    for t in tables:
        nblk = t.shape[0] // tm
        in_specs.append(pl.BlockSpec((tm, LANES), lambda i, nblk=nblk: (i % nblk, 0)))
        args.append(t)
    for c in consts[3:]:
        in_specs.append(_full_spec(c.shape))
        args.append(c)
    n_in = len(args)
    out_specs = [_row_spec(tm * r, n) for r, n, _ in outs]
    out_shape = [jax.ShapeDtypeStruct((T * r, n), dt) for r, n, dt in outs]
    aliases = {}
    if stacked is not None:
        slot, n_slots, stacks = stacked
        for oi, prev in stacks.items():
            r, n, dt = outs[oi]
            out_specs[oi] = pl.BlockSpec((None, tm * r, n), lambda i: (slot, i, 0))
            out_shape[oi] = jax.ShapeDtypeStruct((n_slots, T * r, n), dt)
            if prev is not None:
                aliases[len(args)] = oi
                in_specs.append(pl.BlockSpec(memory_space=pl.ANY))
                args.append(prev)
    n_alias = len(args) - n_in

    def body(*refs):
        kern(*refs[:n_in], *refs[n_in + n_alias:])

    return pl.pallas_call(
        body,
        grid=(T // tm,),
        in_specs=in_specs,
        out_specs=out_specs,
        out_shape=out_shape,
        input_output_aliases=aliases,
        compiler_params=_params("parallel"),
    )(*args)


def _rope_tables(pos, head_dim, tm):
    half = head_dim // 2
    inv = ROPE_THETA ** (-jnp.arange(half, dtype=F32) / half)
    ang = pos.astype(F32)[:, None] * inv[None, :]
    cos, sin = jnp.cos(ang), jnp.sin(ang)
    reps = LANES // head_dim
    cf = jnp.tile(jnp.concatenate([cos, cos], axis=-1), (1, reps))
    ss = jnp.tile(jnp.concatenate([-sin, sin], axis=-1), (1, reps))
    if cf.shape[0] < tm:
        cf = jnp.tile(cf, (tm // cf.shape[0], 1))
        ss = jnp.tile(ss, (tm // ss.shape[0], 1))
    return cf, ss


def _out_kernel(*refs, gated, final):
    refs = list(refs)
    o_ref = refs.pop(0)
    g_ref = refs.pop(0) if gated else None
    x_ref, w_ref = refs.pop(0), refs.pop(0)
    nf_ref = refs.pop(0) if final else None
    y_ref = refs.pop(0)
    o = o_ref[...]
    if gated:
        o = (o.astype(F32) * _silu(g_ref[...].astype(F32))).astype(BF16)
    y = x_ref[...] + _dot(o, w_ref[...])
    if final:
        ms = jnp.mean(y * y, axis=-1, keepdims=True)
        y = y * lax.rsqrt(ms + EPS) * nf_ref[...]
    y_ref[...] = y


def _out_proj(o, g, x2d, w_bf16, norm_f, tm):
    T = x2d.shape[0]
    gated, final = g is not None, norm_f is not None
    args, specs = [o], [_row_spec(tm, BRANCH)]
    if gated:
        args.append(g)
        specs.append(_row_spec(tm, BRANCH))
    args += [x2d, w_bf16]
    specs += [_row_spec(tm, D_MODEL), _full_spec(w_bf16.shape)]
    if final:
        args.append(norm_f.reshape(1, D_MODEL))
        specs.append(_full_spec((1, D_MODEL)))
    return pl.pallas_call(
        functools.partial(_out_kernel, gated=gated, final=final),
        grid=(T // tm,), in_specs=specs, out_specs=_row_spec(tm, D_MODEL),
        out_shape=jax.ShapeDtypeStruct((T, D_MODEL), F32),
        compiler_params=_params("parallel"),
    )(*args)


def _cumsum_kernel(x_ref, o_ref, *, n_blk):
    tril = _tril(PAGE).astype(F32)
    carry = jnp.zeros((1, N_HEADS), F32)
    for c in range(n_blk):
        y = _dot(tril, x_ref[0, c * PAGE:(c + 1) * PAGE, :], HIGHEST) + carry
        o_ref[0, c * PAGE:(c + 1) * PAGE, :] = y
        carry = y[PAGE - 1:PAGE, :]


def _cumsum_seq(logf):
    B, L, H = logf.shape
    return pl.pallas_call(
        functools.partial(_cumsum_kernel, n_blk=L // PAGE),
        grid=(B,),
        in_specs=[pl.BlockSpec((1, L, H), lambda b: (b, 0, 0))],
        out_specs=pl.BlockSpec((1, L, H), lambda b: (b, 0, 0)),
        out_shape=jax.ShapeDtypeStruct((B, L, H), F32),
        compiler_params=_params("parallel"),
    )(logf)


def _softmax_step(s, m, l, acc, pv_fn):
    m_new = jnp.maximum(m, jnp.max(s, axis=-1, keepdims=True))
    alpha = jnp.exp(m - m_new)
    p = jnp.exp(s - m_new)
    l = alpha * l + jnp.sum(p, axis=-1, keepdims=True)
    acc = alpha * acc + pv_fn(p.astype(BF16))
    return m_new, l, acc


def _fox_attn_kernel(q_ref, k_ref, v_ref, ck_ref, o_ref, *, tq, tk, ts):
    i = pl.program_id(2)
    n_sub = tq // ts
    qs = [q_ref[0, r * ts:(r + 1) * ts, :] for r in range(n_sub)]

    def step(j, carry, masked):
        start = pl.multiple_of(j * tk, tk)
        k = k_ref[0, pl.ds(start, tk), :]
        v = v_ref[0, pl.ds(start, tk), :]
        ck = ck_ref[0, 0, j]
        out = []
        for r in range(n_sub):
            s = _dot_nt(qs[r], k) - ck
            if masked:
                s = jnp.where(start + _iota((ts, tk), 1) <= i * tq + r * ts + _iota((ts, tk), 0), s, NEG)
            out.append(_softmax_step(s, *carry[r], lambda p: _dot(p, v)))
        return tuple(out)

    carry = tuple((jnp.full((ts, 1), NEG, F32), jnp.zeros((ts, 1), F32), jnp.zeros((ts, HEAD_DIM), F32))
                  for _ in range(n_sub))
    n_diag = tq // tk
    carry = lax.fori_loop(0, i * n_diag, lambda j, c: step(j, c, False), carry)
    for d in range(n_diag):
        carry = step(i * n_diag + d, carry, True)
    for r, (_, l, acc) in enumerate(carry):
        o_ref[0, r * ts:(r + 1) * ts, :] = (acc / l).astype(BF16)


def _fox_attn_prompt(q, k, v, cum, tq, tk, ts):
    B, L, _ = q.shape
    nb = L // tq
    ck = jnp.swapaxes(cum, 1, 2).reshape(B, N_HEADS, L // tk, 1, tk)
    return pl.pallas_call(
        functools.partial(_fox_attn_kernel, tq=tq, tk=tk, ts=ts),
        grid=(B, N_HEADS, nb),
        in_specs=[
            pl.BlockSpec((1, tq, HEAD_DIM), lambda b, h, i: (b, i, h)),
            pl.BlockSpec((1, L, HEAD_DIM), lambda b, h, i: (b, 0, h)),
            pl.BlockSpec((1, L, HEAD_DIM), lambda b, h, i: (b, 0, h)),
            pl.BlockSpec((1, 1, L // tk, 1, tk), lambda b, h, i: (b, h, 0, 0, 0)),
        ],
        out_specs=pl.BlockSpec((1, tq, HEAD_DIM), lambda b, h, i: (b, i, h)),
        out_shape=jax.ShapeDtypeStruct((B, L, BRANCH), BF16),
        compiler_params=_params("parallel", "parallel", "arbitrary"),
    )(q, k, v, ck)


def _cumsum_lanes(x):
    lane = _iota(x.shape, 1)
    d = 1
    while d < x.shape[1]:
        x = x + jnp.where(lane >= d, pltpu.roll(x, d, 1), 0.0)
        d *= 2
    return x


def _fox_dec_kernel(pt_ref, q_ref, kn_ref, vn_ref, lfn_ref, *rest, pps, n_steps, dl):
    k_refs, v_refs, lf_refs = rest[0:pps], rest[pps:2 * pps], rest[2 * pps:3 * pps]
    o_ref = rest[3 * pps]
    m_ref, l_ref, acc_ref, car_ref = rest[3 * pps + 1:]
    p = pl.program_id(1)
    nr = N_HEADS * dl
    n_chain = m_ref.shape[0]

    @pl.when(p == 0)
    def _():
        m_ref[...] = jnp.full(m_ref.shape, NEG, F32)
        l_ref[...] = jnp.zeros(l_ref.shape, F32)
        acc_ref[...] = jnp.zeros(acc_ref.shape, F32)
        car_ref[...] = jnp.zeros((N_HEADS, LANES), F32)

    qs = [q_ref[0, :, h * HEAD_DIM:(h + 1) * HEAD_DIM] for h in range(N_HEADS)]

    def attend(chains, n_tok, mask):
        off, cums = car_ref[...], {}
        for ci, (_, blocks) in enumerate(chains):
            for bi, (_, _, lf) in enumerate(blocks):
                c = _cumsum_lanes(lf)
                cums[ci, bi] = (c + off)[:, 0:n_tok]
                off = off + jnp.broadcast_to(c[:, LANES - 1:LANES], (N_HEADS, LANES))
        car_ref[...] = off

        def head_rows(blocks, which, h):
            return jnp.concatenate([blk[which][_head_rows(h, n_tok, N_HEADS), :].astype(BF16) for blk in blocks],
                                   axis=0)

        ss = []
        for ci, (_, blocks) in enumerate(chains):
            cum = jnp.concatenate([cums[ci, bi] for bi in range(len(blocks))], axis=1)
            s = jnp.concatenate([_dot_nt(qs[h], head_rows(blocks, 0, h)) - cum[h:h + 1, :] for h in range(N_HEADS)],
                                axis=0)
            ss.append(s if mask is None else jnp.where(mask, s, NEG))
        for (st, blocks), s in zip(chains, ss):
            def pv(pb, blocks=blocks):
                return jnp.concatenate([_dot(pb[h * dl:(h + 1) * dl], head_rows(blocks, 1, h))
                                        for h in range(N_HEADS)], axis=0)
            m_ref[st], l_ref[st], acc_ref[st] = _softmax_step(s, m_ref[st], l_ref[st], acc_ref[st], pv)

    per_chain = pps // n_chain
    attend([(ci, [(k_refs[r], v_refs[r], lf_refs[r][...]) for r in range(ci * per_chain, (ci + 1) * per_chain)])
            for ci in range(n_chain)], PAGE, None)

    @pl.when(p == n_steps - 1)
    def _():
        causal = _iota((nr, dl), 1) <= _iota((nr, dl), 0) % dl
        attend([(0, [(kn_ref, vn_ref, lfn_ref[0])])], dl, causal)
        m = m_ref[0]
        for ci in range(1, n_chain):
            m = jnp.maximum(m, m_ref[ci])
        l, acc = jnp.zeros((nr, 1), F32), jnp.zeros((nr, HEAD_DIM), F32)
        for ci in range(n_chain):
            w = jnp.exp(m_ref[ci] - m)
            l, acc = l + w * l_ref[ci], acc + w * acc_ref[ci]
        out = acc / l
        for h in range(N_HEADS):
            o_ref[0, :, h * HEAD_DIM:(h + 1) * HEAD_DIM] = out[h * dl:(h + 1) * dl].astype(BF16)


def _fox_attn_decode(q, kn, vn, lfn, cache_k, cache_v, cache_lf, layer, pt_flat, n_pages, pps):
    DB, DL, _ = q.shape
    n_steps = n_pages // pps
    nr = N_HEADS * DL
    n_chain = max(pps // 4, 1)
    lfn_t =jnp.pad(jnp.swapaxes(lfn, 1, 2), ((0, 0), (0, 0), (0, LANES - DL)))

    def page_spec(rows, r):
        return pl.BlockSpec((None, None, rows, LANES),
                            lambda b, p, pt, r=r: (layer, pt[b * n_pages + p * pps + r], 0, 0))

    new_rows = pl.BlockSpec((DL * N_HEADS, HEAD_DIM), lambda b, p, pt: (b, 0))
    tok = lambda n, w: pl.BlockSpec((1, n, w), lambda b, p, pt: (b, 0, 0))
    in_specs = [tok(DL, BRANCH), new_rows, new_rows, tok(N_HEADS, LANES)]
    in_specs += [page_spec(PAGE * N_HEADS, r) for r in range(pps)] * 2 + [page_spec(N_HEADS, r) for r in range(pps)]
    grid_spec = pltpu.PrefetchScalarGridSpec(
        num_scalar_prefetch=1, grid=(DB, n_steps), in_specs=in_specs, out_specs=tok(DL, BRANCH),
        scratch_shapes=[pltpu.VMEM((n_chain, nr, 1), F32), pltpu.VMEM((n_chain, nr, 1), F32),
                        pltpu.VMEM((n_chain, nr, HEAD_DIM), F32), pltpu.VMEM((N_HEADS, LANES), F32)])
    return pl.pallas_call(
        functools.partial(_fox_dec_kernel, pps=pps, n_steps=n_steps, dl=DL),
        grid_spec=grid_spec,
        out_shape=jax.ShapeDtypeStruct((DB, DL, BRANCH), BF16),
        compiler_params=_params("parallel", "arbitrary"),
    )(pt_flat, q, kn, vn, lfn_t, *([cache_k] * pps), *([cache_v] * pps), *([cache_lf] * pps))


def _key_to_float(t):
    return pltpu.bitcast(t ^ ((t >> 31) & 0x7FFFFFFF), F32)


def _kth_largest(count_ge, shape, n_sel, bits_per_step):
    def step(si, t):
        shift = 32 - bits_per_step * (si + 1)
        best = t
        for j in range(1, 2 ** bits_per_step):
            cand = t ^ lax.shift_left(jnp.int32(j), shift)
            best = jnp.where(count_ge(_key_to_float(cand)) >= n_sel, jnp.maximum(best, cand), best)
        return best
    t = lax.fori_loop(0, 32 // bits_per_step, step, jnp.full(shape, INT_MIN, jnp.int32))
    return _key_to_float(jnp.maximum(t, NEG_INF_KEY + 1))


def _strict_upper(n):
    return (_iota((n, n), 0) < _iota((n, n), 1)).astype(BF16)


def _dsa_attn_kernel(q_ref, qi_ref, wi_ref, k_ref, v_ref, ki_ref, o_ref, sc_ref, sct_ref, wb_ref, *, n_sel, tq, W):
    i = pl.program_id(1)
    nch = (i * tq + tq + W - 1) // W
    reps = W // LANES
    wi = wi_ref[0]
    for h in range(IDX_HEADS):
        wb_ref[h] = jnp.broadcast_to(wi[:, h:h + 1], (tq, LANES))
    row, col = _iota((tq, W), 0), _iota((tq, W), 1)

    def fill(c, _):
        kc = ki_ref[0, pl.ds(pl.multiple_of(c * W, W), W), :].astype(BF16)
        sc = jnp.zeros((tq, W), F32)
        for h in range(IDX_HEADS):
            d = _dot_nt(qi_ref[0, :, h * IDX_DIM:(h + 1) * IDX_DIM], kc)
            sc = sc + jnp.concatenate([wb_ref[h]] * reps, axis=1) * jnp.maximum(d, 0.0)
        sc = jnp.where(c * W + col <= i * tq + row, sc, -jnp.inf)
        sc_ref[c] = sc
        sct_ref[c] = sc.T
        return 0

    lax.fori_loop(0, nch, fill, 0)

    def count(pred):
        def body(c, acc):
            x = jnp.where(pred(sc_ref[c]), 1.0, 0.0)
            for r in range(reps):
                acc = acc + x[:, r * LANES:(r + 1) * LANES]
            return acc
        return jnp.sum(lax.fori_loop(0, nch, body, jnp.zeros((tq, LANES), F32)), axis=-1, keepdims=True)

    def count_ge_t(f):
        f8 = jnp.broadcast_to(f, (SUBLANES, tq))
        n_acc = 4

        def body(c, accs):
            accs = list(accs)
            for r in range(W // SUBLANES):
                x = jnp.where(sct_ref[c, r * SUBLANES:(r + 1) * SUBLANES, :] >= f8, 1.0, 0.0)
                accs[r % n_acc] = accs[r % n_acc] + x
            return tuple(accs)

        accs = lax.fori_loop(0, nch, body, tuple(jnp.zeros((SUBLANES, tq), F32) for _ in range(n_acc)))
        return jnp.sum((accs[0] + accs[1]) + (accs[2] + accs[3]), axis=0, keepdims=True)

    thr_row = _kth_largest(count_ge_t, (1, tq), n_sel, 1)
    eye = _iota((tq, tq), 0) == _iota((tq, tq), 1)
    thr = jnp.sum(jnp.where(eye, jnp.broadcast_to(thr_row, (tq, tq)), 0.0), axis=-1, keepdims=True)

    @pl.when(jnp.max(count(lambda s: s >= thr)) > n_sel)
    def _():
        need = n_sel - count(lambda s: s > thr)
        upper = _strict_upper(W)

        def body(c, seen):
            sc = sc_ref[c]
            eq = sc == thr
            eqf = jnp.where(eq, 1.0, 0.0)
            rank = seen + _dot(eqf.astype(BF16), upper)
            sc_ref[c] = jnp.where(eq & (rank >= need), -jnp.inf, sc)
            return seen + jnp.sum(eqf, axis=-1, keepdims=True)

        lax.fori_loop(0, nch, body, jnp.zeros((tq, 1), F32))

    qgs = [jnp.concatenate(
        [q_ref[0, :, (g * DSA_GROUP + r) * HEAD_DIM:(g * DSA_GROUP + r + 1) * HEAD_DIM] for r in range(DSA_GROUP)],
        axis=0) for g in range(DSA_KV_HEADS)]

    def body(c, carry):
        start = pl.multiple_of(c * W, W)
        bias = jnp.where(sc_ref[c] >= thr, 0.0, NEG)[None]
        out = []
        for g in range(DSA_KV_HEADS):
            m, l, acc = carry[g]
            gsl = slice(g * HEAD_DIM, (g + 1) * HEAD_DIM)
            s = _dot_nt(qgs[g], k_ref[0, pl.ds(start, W), gsl]).reshape(DSA_GROUP, tq, W) + bias
            m_new = jnp.maximum(m, jnp.max(s, axis=-1, keepdims=True))
            alpha = jnp.exp(m - m_new)
            p = jnp.exp(s - m_new)
            l = alpha * l + jnp.sum(p, axis=-1, keepdims=True)
            pv = _dot(p.reshape(DSA_GROUP * tq, W).astype(BF16), v_ref[0, pl.ds(start, W), gsl])
            out.append((m_new, l, alpha * acc + pv.reshape(DSA_GROUP, tq, HEAD_DIM)))
        return tuple(out)

    init = tuple((jnp.full((DSA_GROUP, tq, 1), NEG, F32), jnp.zeros((DSA_GROUP, tq, 1), F32),
                  jnp.zeros((DSA_GROUP, tq, HEAD_DIM), F32)) for _ in range(DSA_KV_HEADS))
    res = lax.fori_loop(0, nch, body, init)
    for g, (_, l, acc) in enumerate(res):
        out = acc / l
        for r in range(DSA_GROUP):
            hh = g * DSA_GROUP + r
            o_ref[0, :, hh * HEAD_DIM:(hh + 1) * HEAD_DIM] = out[r].astype(BF16)


def _dsa_attn_prompt(q, qi, wi, k, v, ki, n_sel, tq):
    B, L, _ = q.shape
    nb = L // tq
    W = min(4 * LANES, L)
    blk = lambda w: pl.BlockSpec((1, tq, w), lambda b, i: (b, i, 0))
    full = lambda w: pl.BlockSpec((1, L, w), lambda b, i: (b, 0, 0))
    return pl.pallas_call(
        functools.partial(_dsa_attn_kernel, n_sel=n_sel, tq=tq, W=W),
        grid=(B, nb),
        in_specs=[blk(BRANCH), blk(IDX_HEADS * IDX_DIM), blk(IDX_HEADS), full(DSA_KV), full(DSA_KV), full(IDX_DIM)],
        out_specs=blk(BRANCH),
        out_shape=jax.ShapeDtypeStruct((B, L, BRANCH), BF16),
        scratch_shapes=[pltpu.VMEM((L // W, tq, W), F32), pltpu.VMEM((L // W, W, tq), F32),
                        pltpu.VMEM((IDX_HEADS, tq, LANES), F32)],
        compiler_params=_params("parallel", "arbitrary"),
    )(q, qi, wi, k, v, ki)


def _dsa_dec_kernel(pt_ref, q_ref, qi_ref, wi_ref, kn_ref, vn_ref, kin_ref, *rest, pps, n_pages, dl, n_sel, bb):
    n_pg = bb * pps
    k_refs, v_refs, ki_refs = rest[0:n_pg], rest[n_pg:2 * n_pg], rest[2 * n_pg:3 * n_pg]
    o_ref = rest[3 * n_pg]
    ks_ref, vs_ref, kis_ref = rest[3 * n_pg + 1:]
    p = pl.program_id(1)
    past = n_pages * PAGE
    nk = past + PAGE
    nr = bb * dl
    G = DSA_KV_HEADS

    for e in range(bb):
        for r in range(pps):
            page = p * pps + r
            start = pl.multiple_of(page * (PAGE * G), PAGE * G)
            ks_ref[e, pl.ds(start, PAGE * G), :] = k_refs[e * pps + r][...]
            vs_ref[e, pl.ds(start, PAGE * G), :] = v_refs[e * pps + r][...]
            kis_ref[e, page] = ki_refs[e * pps + r][...]

    @pl.when(p == n_pages // pps - 1)
    def _():
        for e in range(bb):
            for ref, new in ((ks_ref, kn_ref), (vs_ref, vn_ref)):
                ref[e, past * G:(past + dl) * G, :] = new[e * dl * G:(e + 1) * dl * G, :]
                ref[e, (past + dl) * G:nk * G, :] = jnp.zeros(((PAGE - dl) * G, HEAD_DIM), F32)
            kis_ref[e, n_pages] = kin_ref[e]

        rows = []
        for e in range(bb):
            qi, wi = qi_ref[e], wi_ref[e]
            chunks = []
            for c in range(n_pages + 1):
                wr = wi * jnp.maximum(_dot(qi, kis_ref[e, c].astype(BF16)), 0.0)
                sc = wr[0:dl]
                for h in range(1, IDX_HEADS):
                    sc = sc + wr[h * dl:(h + 1) * dl]
                chunks.append(sc)
            rows.append(jnp.concatenate(chunks, axis=1))
        score = jnp.concatenate(rows, axis=0)
        valid = _iota((nr, nk), 1) <= past + _iota((nr, nk), 0) % dl
        score = jnp.where(valid, score, -jnp.inf)

        def count(mask):
            return jnp.sum(jnp.where(mask, 1.0, 0.0), axis=-1, keepdims=True)

        thr = _kth_largest(lambda f: count(score >= f), (nr, 1), n_sel, 2)

        def drop_surplus_ties(sc):
            need = n_sel - count(sc > thr)
            upper = _strict_upper(PAGE)
            seen, kept = jnp.zeros((nr, 1), F32), []
            for c in range(n_pages + 1):
                scc = sc[:, c * PAGE:(c + 1) * PAGE]
                eq = scc == thr
                eqf = jnp.where(eq, 1.0, 0.0)
                rank = seen + _dot(eqf.astype(BF16), upper)
                kept.append(jnp.where(eq & (rank >= need), -jnp.inf, scc))
                seen = seen + jnp.sum(eqf, axis=-1, keepdims=True)
            return jnp.concatenate(kept, axis=1)

        score = lax.cond(jnp.max(count(score >= thr)) > n_sel, drop_surplus_ties, lambda sc: sc, score)
        sel = score >= thr

        units = [(e, g) for e in range(bb) for g in range(G)]

        def stage(fn):
            return [fn(i, e, g) for i, (e, g) in enumerate(units)]

        qg = stage(lambda i, e, g: jnp.concatenate(
            [q_ref[e, :, (g * DSA_GROUP + r) * HEAD_DIM:(g * DSA_GROUP + r + 1) * HEAD_DIM].astype(F32)
             for r in range(DSA_GROUP)], axis=0).astype(BF16))
        s = stage(lambda i, e, g: jnp.where(
            sel[e * dl:(e + 1) * dl][None],
            _dot_nt(qg[i], ks_ref[e, _head_rows(g, nk, G), :].astype(BF16)).reshape(DSA_GROUP, dl, nk), NEG))
        pr = stage(lambda i, e, g: jnp.exp(s[i] - jnp.max(s[i], axis=-1, keepdims=True)))
        den = stage(lambda i, e, g: jnp.sum(pr[i], axis=-1, keepdims=True))
        pv = stage(lambda i, e, g: _dot(pr[i].reshape(DSA_GROUP * dl, nk).astype(BF16),
                                        vs_ref[e, _head_rows(g, nk, G), :].astype(BF16)))
        for i, (e, g) in enumerate(units):
            out = pv[i].reshape(DSA_GROUP, dl, HEAD_DIM) / den[i]
            for r in range(DSA_GROUP):
                hh = g * DSA_GROUP + r
                o_ref[e, :, hh * HEAD_DIM:(hh + 1) * HEAD_DIM] = out[r].astype(BF16)


def _dsa_attn_decode(q, qi, wi, kn, vn, kin, cache_k, cache_v, cache_ki, layer, pt_flat, n_pages, pps, n_sel):
    DB, DL, _ = q.shape
    nk = n_pages * PAGE + PAGE
    G = DSA_KV_HEADS
    bb = max(d for d in (1, 2, 4) if DB % d == 0)
    qi_hq = jnp.swapaxes(qi.reshape(DB, DL, IDX_HEADS, IDX_DIM), 1, 2).reshape(DB, IDX_HEADS * DL, IDX_DIM)
    wi_hq = jnp.swapaxes(wi.reshape(DB, DL, IDX_HEADS), 1, 2).reshape(DB, IDX_HEADS * DL, 1)
    kin_t = jnp.pad(jnp.swapaxes(kin, 1, 2), ((0, 0), (0, 0), (0, PAGE - DL)))

    def page_specs(rows):
        return [pl.BlockSpec((None, None, rows, LANES),
                             lambda b, p, pt, e=e, r=r: (layer, pt[(b * bb + e) * n_pages + p * pps + r], 0, 0))
                for e in range(bb) for r in range(pps)]

    tok = lambda n, w: pl.BlockSpec((bb, n, w), lambda b, p, pt: (b, 0, 0))
    new_rows = pl.BlockSpec((bb * DL * G, HEAD_DIM), lambda b, p, pt: (b, 0))
    in_specs = [tok(DL, BRANCH), tok(IDX_HEADS * DL, IDX_DIM), tok(IDX_HEADS * DL, 1),
                new_rows, new_rows, tok(IDX_DIM, PAGE)]
    in_specs += page_specs(PAGE * G) * 2 + page_specs(IDX_DIM)
    n_pg = bb * pps
    grid_spec = pltpu.PrefetchScalarGridSpec(
        num_scalar_prefetch=1, grid=(DB // bb, n_pages // pps), in_specs=in_specs, out_specs=tok(DL, BRANCH),
        scratch_shapes=[pltpu.VMEM((bb, nk * G, HEAD_DIM), F32), pltpu.VMEM((bb, nk * G, HEAD_DIM), F32),
                        pltpu.VMEM((bb, n_pages + 1, IDX_DIM, PAGE), F32)])
    return pl.pallas_call(
        functools.partial(_dsa_dec_kernel, pps=pps, n_pages=n_pages, dl=DL, n_sel=n_sel, bb=bb),
        grid_spec=grid_spec,
        out_shape=jax.ShapeDtypeStruct((DB, DL, BRANCH), BF16),
        compiler_params=_params("parallel", "arbitrary"),
    )(pt_flat, q, qi_hq, wi_hq, kn, vn, kin_t, *([cache_k] * n_pg), *([cache_v] * n_pg), *([cache_ki] * n_pg))


def _gdn_kernel(x_ref, z_ref, g_ref, b_ref, cw_ref, nw_ref, cs_ref, s0_ref, o_ref, sout_ref, xb_ref, s_ref,
                *, C, nc, bb):
    c = pl.program_id(1)
    HIST = SUBLANES

    @pl.when(c == 0)
    def _():
        xb_ref[:, 0:HIST, :] = cs_ref[...]
        s_ref[...] = s0_ref[...]

    incl, strict = _tril(C), _tril(C, strict=True)
    eye_c = (_iota((C, C), 0) == _iota((C, C), 1)).astype(F32)
    eye_h = (_iota((N_HEADS, N_HEADS), 0) == _iota((N_HEADS, N_HEADS), 1)).astype(F32)
    n_dbl = int(math.log2(C)) - 1

    convs, G_alls, Gt_alls = [], [], []
    for e in range(bb):
        xb_ref[e, HIST:HIST + C, :] = x_ref[e]
        conv = xb_ref[e, HIST:HIST + C, :] * cw_ref[CONV_W - 1:CONV_W, :]
        for j in range(CONV_W - 1):
            off = HIST - (CONV_W - 1) + j
            conv = conv + xb_ref[e, off:off + C, :] * cw_ref[j:j + 1, :]
        hist = xb_ref[e, C:C + HIST, :]
        xb_ref[e, 0:HIST, :] = hist
        convs.append(_silu(conv))
        G = _dot(incl.astype(F32), g_ref[e], HIGHEST)
        G_alls.append(G)
        Gt_alls.append(_dot_nt(eye_h, G, HIGHEST))

    units = [(e, h) for e in range(bb) for h in range(N_HEADS)]

    def stage(fn):
        return [fn(i, e, h) for i, (e, h) in enumerate(units)]

    def head(e, j, h):
        return convs[e][:, j * BRANCH + h * HEAD_DIM:j * BRANCH + (h + 1) * HEAD_DIM]

    def l2n(a):
        return a * lax.rsqrt(jnp.sum(a * a, axis=-1, keepdims=True) + EPS)

    q = stage(lambda i, e, h: l2n(head(e, 0, h)) * ATTN_SCALE)
    k = stage(lambda i, e, h: l2n(head(e, 1, h)))
    v = stage(lambda i, e, h: head(e, 2, h))
    Gc = stage(lambda i, e, h: G_alls[e][:, h:h + 1])
    bc = stage(lambda i, e, h: b_ref[e][:, h:h + 1])
    decay = stage(lambda i, e, h: jnp.exp(jnp.where(incl, Gc[i] - Gt_alls[e][h:h + 1, :], -jnp.inf)))
    kqk = stage(lambda i, e, h: _dot1(jnp.concatenate([k[i], q[i]], axis=0), k[i], _dot_nt))
    A = stage(lambda i, e, h: jnp.where(strict, bc[i] * kqk[i][0:C] * decay[i], 0.0))
    T = stage(lambda i, e, h: eye_c - A[i])
    P = stage(lambda i, e, h: _dot1(A[i], A[i]))
    for it in range(n_dbl):
        TP = stage(lambda i, e, h: _dot1(T[i], P[i]))
        T = stage(lambda i, e, h: T[i] + TP[i])
        if it + 1 < n_dbl:
            P = stage(lambda i, e, h: _dot1(P[i], P[i]))
    eG = stage(lambda i, e, h: jnp.exp(Gc[i]))
    WU = stage(lambda i, e, h: _dot1(T[i], jnp.concatenate([k[i] * (bc[i] * eG[i]), v[i] * bc[i]], axis=1)))
    Aqk = stage(lambda i, e, h: kqk[i][C:2 * C] * decay[i])
    G_last = stage(lambda i, e, h: Gc[i][C - 1:C, :])
    k_dec = stage(lambda i, e, h: k[i] * jnp.exp(G_last[i] - Gc[i]))
    S = stage(lambda i, e, h: s_ref[e, h])
    WqS = stage(lambda i, e, h: _dot1(jnp.concatenate([WU[i][:, 0:HEAD_DIM], q[i]], axis=0), S[i]))
    Vn = stage(lambda i, e, h: WU[i][:, HEAD_DIM:2 * HEAD_DIM] - WqS[i][0:C])
    o = stage(lambda i, e, h: eG[i] * WqS[i][C:2 * C] + _dot1(Aqk[i], Vn[i]))
    S_new = stage(lambda i, e, h: jnp.exp(G_last[i]) * S[i] + _dot1(k_dec[i], Vn[i], _dot_tn))
    for i, (e, h) in enumerate(units):
        sl = slice(h * HEAD_DIM, (h + 1) * HEAD_DIM)
        s_ref[e, h] = S_new[i]
        on = o[i] * lax.rsqrt(jnp.mean(o[i] * o[i], axis=-1, keepdims=True) + EPS) * nw_ref[...]
        o_ref[e, :, sl] = (on * _silu(z_ref[e, :, sl].astype(F32))).astype(BF16)

    @pl.when(c == nc - 1)
    def _():
        sout_ref[...] = s_ref[...]


def _gdn_mix(qkv, z, g, beta, conv_w, norm_w, conv_state, S0):
    B, L, _ = qkv.shape
    C = min(GDN_CHUNK, L)
    nc = L // C
    assert C & (C - 1) == 0 and L % C == 0
    bb = max(d for d in (1, 2, 4) if B % d == 0 and d * C <= 2 * GDN_CHUNK)
    cs = jnp.pad(conv_state, ((0, 0), (SUBLANES - (CONV_W - 1), 0), (0, 0)))
    blk = lambda w: pl.BlockSpec((bb, C, w), lambda b, c: (b, c, 0))
    state = pl.BlockSpec((bb, N_HEADS, HEAD_DIM, HEAD_DIM), lambda b, c: (b, 0, 0, 0))
    return pl.pallas_call(
        functools.partial(_gdn_kernel, C=C, nc=nc, bb=bb),
        grid=(B // bb, nc),
        in_specs=[blk(3 * BRANCH), blk(BRANCH), blk(N_HEADS), blk(N_HEADS),
                  pl.BlockSpec((CONV_W, 3 * BRANCH), lambda b, c: (0, 0)),
                  pl.BlockSpec((1, HEAD_DIM), lambda b, c: (0, 0)),
                  pl.BlockSpec((bb, SUBLANES, 3 * BRANCH), lambda b, c: (b, 0, 0)),
                  state],
        out_specs=[blk(BRANCH), state],
        out_shape=[jax.ShapeDtypeStruct((B, L, BRANCH), BF16),
                   jax.ShapeDtypeStruct((B, N_HEADS, HEAD_DIM, HEAD_DIM), F32)],
        scratch_shapes=[pltpu.VMEM((bb, C + SUBLANES, 3 * BRANCH), F32),
                        pltpu.VMEM((bb, N_HEADS, HEAD_DIM, HEAD_DIM), F32)],
        compiler_params=_params("parallel", "arbitrary"),
    )(qkv, z, g, beta, conv_w, norm_w.reshape(1, HEAD_DIM), cs, S0)


def _pad_cols(w, n=LANES):
    return jnp.pad(w, ((0, 0), (0, n - w.shape[1])))


def _row_tile(T):
    return 256 if T % 256 == 0 else T


def kernel(x_prompt, x_sample, cache_fox_k, cache_fox_v, cache_fox_logf, cache_dsa_k, cache_dsa_v, cache_dsa_kidx, state_gdn_conv, state_gdn_S, page_table, norm_w, norm_f, fox_w_in, fox_b_f, fox_w_out, dsa_w_in, dsa_w_out, gdn_w_in, gdn_conv_w, gdn_A_log, gdn_dt_bias, gdn_norm_w, gdn_w_out):
    B, L, _ = x_prompt.shape
    DB, DL, _ = x_sample.shape
    n_pages = page_table.shape[1]
    past = n_pages * PAGE
    depth = norm_w.shape[0]
    pt_flat = page_table.reshape(-1)
    pages_per_step = lambda want: max(p for p in (1, 2, 4, 8, 16) if p <= want and n_pages % p == 0)
    fox_pps, dsa_pps = pages_per_step(16), pages_per_step(4)
    groups = ((B, L), (DB, DL))
    xs = [x_prompt.reshape(B * L, D_MODEL), x_sample.reshape(DB * DL, D_MODEL)]
    tms = [_row_tile(nb * nl) for nb, nl in groups]
    pos = [jnp.arange(L), past + jnp.arange(DL)]
    tq_p = max(t for t in (PAGE, 2 * PAGE, 4 * PAGE) if L % t == 0)

    pool = cache_fox_k.shape[1]
    fox_ck = cache_fox_k.reshape(-1, pool, PAGE * N_HEADS, HEAD_DIM)
    fox_cv = cache_fox_v.reshape(-1, pool, PAGE * N_HEADS, HEAD_DIM)
    fox_clf = jnp.swapaxes(cache_fox_logf, 2, 3)
    dsa_ck = cache_dsa_k.reshape(-1, pool, PAGE * DSA_KV_HEADS, HEAD_DIM)
    dsa_cv = cache_dsa_v.reshape(-1, pool, PAGE * DSA_KV_HEADS, HEAD_DIM)
    dsa_cki = jnp.swapaxes(cache_dsa_kidx, 2, 3)

    st = {name: ([], []) for name in ("fox_k", "fox_v", "fox_lf", "dsa_k", "dsa_v", "dsa_ki", "gdn_conv", "gdn_S")}
    finals = [None, None]
    n_fox = fox_w_in.shape[0]
    fox_kv_stack = [None, None]

    for i in range(depth):
        kind, j = i % 3, i // 3
        nw = norm_w[i].reshape(1, D_MODEL)
        last = i == depth - 1
        for gi, (nb, nl) in enumerate(groups):
            x2d, tm = xs[gi], tms[gi]
            T = nb * nl
            prompt = gi == 0
            sh = lambda a: a.reshape(nb, nl, a.shape[-1])
            if kind == 0:
                w = fox_w_in[j]
                w_main = jnp.concatenate([w[:, :3 * BRANCH], w[:, 3 * BRANCH + N_HEADS:]], axis=1).astype(BF16)
                w_small = _pad_cols(w[:, 3 * BRANCH:3 * BRANCH + N_HEADS]).astype(BF16)
                outs = [(1, BRANCH, BF16), (N_HEADS, HEAD_DIM, F32), (N_HEADS, HEAD_DIM, F32), (1, BRANCH, BF16),
                        (1, N_HEADS, F32)]
                stacked = None
                if prompt:
                    outs += [(1, BRANCH, BF16), (1, BRANCH, BF16)]
                    stacked = (j, n_fox, {1: fox_kv_stack[0], 2: fox_kv_stack[1]})
                res = _in_proj(_fox_in_kernel, x2d, [nw, w_main, w_small, fox_b_f[j].reshape(1, N_HEADS)], [], outs,
                               tm, stacked)
                q, k, v, g, lf = res[:5]
                if prompt:
                    o = _fox_attn_prompt(sh(q), sh(res[5]), sh(res[6]), _cumsum_seq(sh(lf)), tq_p, tq_p, tq_p)
                    fox_kv_stack = [k, v]
                else:
                    o = _fox_attn_decode(sh(q), k, v, sh(lf), fox_ck, fox_cv, fox_clf, j, pt_flat, n_pages, fox_pps)
                    st["fox_k"][gi].append(k.reshape(nb, nl, N_HEADS, HEAD_DIM))
                    st["fox_v"][gi].append(v.reshape(nb, nl, N_HEADS, HEAD_DIM))
                st["fox_lf"][gi].append(lf.reshape(nb, nl, N_HEADS))
                w_out = fox_w_out[j]
            elif kind == 1:
                w = dsa_w_in[j]
                o_qi = BRANCH + 2 * DSA_KV
                o_wi = o_qi + IDX_HEADS * IDX_DIM
                o_ki = o_wi + IDX_HEADS
                o_g = o_ki + IDX_DIM
                w_main = jnp.concatenate([w[:, :o_wi], w[:, o_g:]], axis=1).astype(BF16)
                w_small = _pad_cols(jnp.concatenate([w[:, o_ki:o_g], w[:, o_wi:o_ki]], axis=1)).astype(BF16)
                tables = list(_rope_tables(pos[gi], HEAD_DIM, tm) + _rope_tables(pos[gi], IDX_DIM, tm))
                outs = [(1, BRANCH, BF16), (DSA_KV_HEADS, HEAD_DIM, F32), (DSA_KV_HEADS, HEAD_DIM, F32),
                        (1, IDX_HEADS * IDX_DIM, BF16), (1, BRANCH, BF16), (1, IDX_DIM, F32), (1, IDX_HEADS, F32)]
                if prompt:
                    outs += [(1, DSA_KV, BF16), (1, DSA_KV, BF16)]
                res = _in_proj(_dsa_in_kernel, x2d, [nw, w_main, w_small], tables, outs, tm)
                q, k, v, qi, g, ki, wi = res[:7]
                if prompt:
                    o = _dsa_attn_prompt(sh(q), sh(qi), sh(wi), sh(res[7]), sh(res[8]), sh(ki),
                                         min(TOPK_MAX, L // 4), PAGE)
                else:
                    o = _dsa_attn_decode(sh(q), sh(qi), sh(wi), k, v, sh(ki), dsa_ck, dsa_cv, dsa_cki,
                                         j, pt_flat, n_pages, dsa_pps, min(TOPK_MAX, (past + DL) // 4))
                st["dsa_k"][gi].append(k.reshape(nb, nl, DSA_KV_HEADS, HEAD_DIM))
                st["dsa_v"][gi].append(v.reshape(nb, nl, DSA_KV_HEADS, HEAD_DIM))
                st["dsa_ki"][gi].append(ki.reshape(nb, nl, IDX_DIM))
                w_out = dsa_w_out[j]
            else:
                w = gdn_w_in[j]
                w_main = jnp.concatenate([w[:, :3 * BRANCH], w[:, 3 * BRANCH + 2 * N_HEADS:]], axis=1).astype(BF16)
                w_small = _pad_cols(w[:, 3 * BRANCH:3 * BRANCH + 2 * N_HEADS]).astype(BF16)
                qkv, z, gg, beta = _in_proj(
                    _gdn_in_kernel, x2d,
                    [nw, w_main, w_small, gdn_A_log[j].reshape(1, N_HEADS), gdn_dt_bias[j].reshape(1, N_HEADS)], [],
                    [(1, 3 * BRANCH, F32), (1, BRANCH, BF16), (1, N_HEADS, F32), (1, N_HEADS, F32)], tm)
                if prompt:
                    conv0 = jnp.zeros((nb, CONV_W - 1, 3 * BRANCH), F32)
                    S0 = jnp.zeros((nb, N_HEADS, HEAD_DIM, HEAD_DIM), F32)
                else:
                    conv0, S0 = state_gdn_conv[j], state_gdn_S[j]
                o, S_new = _gdn_mix(sh(qkv), sh(z), sh(gg), sh(beta), gdn_conv_w[j], gdn_norm_w[j], conv0, S0)
                tail = jnp.concatenate([conv0, sh(qkv)[:, -min(nl, CONV_W - 1):]], axis=1)
                st["gdn_conv"][gi].append(tail[:, -(CONV_W - 1):])
                st["gdn_S"][gi].append(S_new)
                g = None
                w_out = gdn_w_out[j]
            y = _out_proj(o.reshape(T, BRANCH), g, x2d, w_out.astype(BF16), norm_f if last else None, tm)
            if last:
                finals[gi] = y.reshape(nb, nl, D_MODEL)
            else:
                xs[gi] = y

    stk = lambda name, gi: jnp.stack(st[name][gi], axis=0)
    fox_kv_p = [a.reshape(n_fox, B, L, N_HEADS, HEAD_DIM) for a in fox_kv_stack]
    return (finals[0], finals[1],
            fox_kv_p[0], fox_kv_p[1], stk("fox_lf", 0),
            stk("fox_k", 1), stk("fox_v", 1), stk("fox_lf", 1),
            stk("dsa_k", 0), stk("dsa_v", 0), stk("dsa_ki", 0),
            stk("dsa_k", 1), stk("dsa_v", 1), stk("dsa_ki", 1),
            stk("gdn_conv", 0), stk("gdn_S", 0),
            stk("gdn_conv", 1), stk("gdn_S", 1))
```

```python
import functools
import math

import jax
import jax.numpy as jnp
from jax import lax
from jax.experimental import pallas as pl
from jax.experimental.pallas import tpu as pltpu

F32 = jnp.float32
BF16 = jnp.bfloat16
HIGHEST = lax.Precision.HIGHEST

D_MODEL = 1024
N_HEADS = 8
HEAD_DIM = 128
BRANCH = N_HEADS * HEAD_DIM
ATTN_SCALE = HEAD_DIM ** -0.5
DSA_KV_HEADS = 2
DSA_GROUP = N_HEADS // DSA_KV_HEADS
DSA_KV = DSA_KV_HEADS * HEAD_DIM
IDX_HEADS = 8
IDX_DIM = 64
IDX_SCALE = IDX_DIM ** -0.5
TOPK_MAX = 256
CONV_W = 4
GDN_CHUNK = 64
ROPE_THETA = 10000.0
EPS = 1e-6
PAGE = 128
LANES = 128
SUBLANES = 8
NEG = -1e30
INT_MIN = -(2 ** 31)
NEG_INF_KEY = INT_MIN + 0x007FFFFF
V7X_VMEM_BYTES = 64 * 1024 * 1024
VMEM_LIMIT = V7X_VMEM_BYTES * 7 // 8


def _dot(a, b, prec=None):
    return jnp.dot(a, b, preferred_element_type=F32, precision=prec)


def _dot_nt(a, b, prec=None):
    return lax.dot_general(a, b, (((1,), (1,)), ((), ())), preferred_element_type=F32, precision=prec)


def _dot_tn(a, b, prec=None):
    return lax.dot_general(a, b, (((0,), (0,)), ((), ())), preferred_element_type=F32, precision=prec)


def _dot1(a, b, dot=_dot):
    return dot(a.astype(BF16), b.astype(BF16))


def _iota(shape, axis):
    return lax.broadcasted_iota(jnp.int32, shape, axis)


def _tril(n, strict=False):
    r, c = _iota((n, n), 0), _iota((n, n), 1)
    return (r > c) if strict else (r >= c)


def _sigmoid(x):
    return 1.0 / (1.0 + jnp.exp(-x))


def _silu(x):
    return x * _sigmoid(x)


def _softplus(x):
    return jnp.maximum(x, 0.0) + jnp.log1p(jnp.exp(-jnp.abs(x)))


def _params(*sem):
    return pltpu.CompilerParams(dimension_semantics=sem, vmem_limit_bytes=VMEM_LIMIT)


def _head_rows(h, n_tok, n_heads):
    return pl.ds(h, n_tok, stride=n_heads)


def _normed(x_ref, nw_ref):
    x = x_ref[...]
    ms = jnp.mean(x * x, axis=-1, keepdims=True)
    return (x * lax.rsqrt(ms + EPS) * nw_ref[...]).astype(BF16)


def _fox_in_kernel(x_ref, nw_ref, w_ref, ws_ref, bf_ref, q_ref, k_ref, v_ref, g_ref, lf_ref, *copies):
    tm = x_ref.shape[0]
    h = _normed(x_ref, nw_ref)
    q_ref[...] = (_dot(h, w_ref[:, 0:BRANCH]) * ATTN_SCALE).astype(BF16)
    for j, ref in ((1, k_ref), (2, v_ref)):
        r = _dot(h, w_ref[:, j * BRANCH:(j + 1) * BRANCH])
        for hh in range(N_HEADS):
            ref[_head_rows(hh, tm, N_HEADS), :] = r[:, hh * HEAD_DIM:(hh + 1) * HEAD_DIM]
        if copies:
            copies[j - 1][...] = r.astype(BF16)
    g_ref[...] = _dot(h, w_ref[:, 3 * BRANCH:4 * BRANCH]).astype(BF16)
    f = _dot(h, ws_ref[...])[:, 0:N_HEADS] + bf_ref[...]
    lf_ref[...] = -_softplus(-f)


def _rope128(x, cos, sin_signed):
    return x * cos + pltpu.roll(x, HEAD_DIM // 2, 1) * sin_signed


def _rope64(x, cos, sin_signed):
    lane = _iota(x.shape, 1)
    rot = jnp.where((lane & (IDX_DIM - 1)) < IDX_DIM // 2,
                    pltpu.roll(x, LANES - IDX_DIM // 2, 1), pltpu.roll(x, IDX_DIM // 2, 1))
    return x * cos + rot * sin_signed


def _dsa_in_kernel(x_ref, nw_ref, w_ref, ws_ref, c128_ref, s128_ref, c64_ref, s64_ref,
                   q_ref, k_ref, v_ref, qi_ref, g_ref, ki_ref, wi_ref, *copies):
    tm = x_ref.shape[0]
    h = _normed(x_ref, nw_ref)
    c128, s128, c64, s64 = c128_ref[...], s128_ref[...], c64_ref[...], s64_ref[...]
    o_k, o_v, o_qi, o_g = BRANCH, BRANCH + DSA_KV, BRANCH + 2 * DSA_KV, BRANCH + 2 * DSA_KV + IDX_HEADS * IDX_DIM
    q_all = _dot(h, w_ref[:, 0:BRANCH])
    for hh in range(N_HEADS):
        sl = slice(hh * HEAD_DIM, (hh + 1) * HEAD_DIM)
        q_ref[:, sl] = (_rope128(q_all[:, sl], c128, s128) * ATTN_SCALE).astype(BF16)
    k_all = _dot(h, w_ref[:, o_k:o_k + DSA_KV])
    v_all = _dot(h, w_ref[:, o_v:o_v + DSA_KV])
    for hh in range(DSA_KV_HEADS):
        sl = slice(hh * HEAD_DIM, (hh + 1) * HEAD_DIM)
        kk = _rope128(k_all[:, sl], c128, s128)
        k_ref[_head_rows(hh, tm, DSA_KV_HEADS), :] = kk
        v_ref[_head_rows(hh, tm, DSA_KV_HEADS), :] = v_all[:, sl]
        if copies:
            copies[0][:, sl] = kk.astype(BF16)
    if copies:
        copies[1][...] = v_all.astype(BF16)
    qi_all = _dot(h, w_ref[:, o_qi:o_qi + IDX_HEADS * IDX_DIM])
    for hh in range(IDX_HEADS * IDX_DIM // LANES):
        sl = slice(hh * LANES, (hh + 1) * LANES)
        qi_ref[:, sl] = _rope64(qi_all[:, sl], c64, s64).astype(BF16)
    g_ref[...] = _dot(h, w_ref[:, o_g:o_g + BRANCH]).astype(BF16)
    small = _dot(h, ws_ref[...])
    ki_ref[...] = _rope64(small, c64, s64)[:, 0:IDX_DIM]
    wi_ref[...] = small[:, IDX_DIM:IDX_DIM + IDX_HEADS] * (IDX_HEADS ** -0.5 * IDX_SCALE)


def _gdn_in_kernel(x_ref, nw_ref, w_ref, ws_ref, alog_ref, dtb_ref, qkv_ref, z_ref, g_ref, beta_ref):
    h = _normed(x_ref, nw_ref)
    for j in range(3):
        sl = slice(j * BRANCH, (j + 1) * BRANCH)
        qkv_ref[:, sl] = _dot(h, w_ref[:, sl])
    z_ref[...] = _dot(h, w_ref[:, 3 * BRANCH:4 * BRANCH]).astype(BF16)
    small = _dot(h, ws_ref[...])
    a = small[:, 0:N_HEADS]
    b = small[:, N_HEADS:2 * N_HEADS]
    g_ref[...] = -jnp.exp(alog_ref[...]) * _softplus(a + dtb_ref[...])
    beta_ref[...] = _sigmoid(b)


def _row_spec(tm, n):
    return pl.BlockSpec((tm, n), lambda i: (i, 0))


def _full_spec(shape):
    return pl.BlockSpec(shape, lambda i: (0,) * len(shape))


def _in_proj(kern, x2d, consts, tables, outs, tm, stacked=None):
    T = x2d.shape[0]
    in_specs = [_row_spec(tm, D_MODEL)] + [_full_spec(c.shape) for c in consts[:3]]
    args = [x2d] + list(consts[:3])
    for t in tables:
        nblk = t.shape[0] // tm
        in_specs.append(pl.BlockSpec((tm, LANES), lambda i, nblk=nblk: (i % nblk, 0)))
        args.append(t)
    for c in consts[3:]:
        in_specs.append(_full_spec(c.shape))
        args.append(c)
    n_in = len(args)
    out_specs = [_row_spec(tm * r, n) for r, n, _ in outs]
    out_shape = [jax.ShapeDtypeStruct((T * r, n), dt) for r, n, dt in outs]
    aliases = {}
    if stacked is not None:
        slot, n_slots, stacks = stacked
        for oi, prev in stacks.items():
            r, n, dt = outs[oi]
            out_specs[oi] = pl.BlockSpec((None, tm * r, n), lambda i: (slot, i, 0))
            out_shape[oi] = jax.ShapeDtypeStruct((n_slots, T * r, n), dt)
            if prev is not None:
                aliases[len(args)] = oi
                in_specs.append(pl.BlockSpec(memory_space=pl.ANY))
                args.append(prev)
    n_alias = len(args) - n_in

    def body(*refs):
        kern(*refs[:n_in], *refs[n_in + n_alias:])

    return pl.pallas_call(
        body,
        grid=(T // tm,),
        in_specs=in_specs,
        out_specs=out_specs,
        out_shape=out_shape,
        input_output_aliases=aliases,
        compiler_params=_params("parallel"),
    )(*args)


def _rope_tables(pos, head_dim, tm):
    half = head_dim // 2
    inv = ROPE_THETA ** (-jnp.arange(half, dtype=F32) / half)
    ang = pos.astype(F32)[:, None] * inv[None, :]
    cos, sin = jnp.cos(ang), jnp.sin(ang)
    reps = LANES // head_dim
    cf = jnp.tile(jnp.concatenate([cos, cos], axis=-1), (1, reps))
    ss = jnp.tile(jnp.concatenate([-sin, sin], axis=-1), (1, reps))
    if cf.shape[0] < tm:
        cf = jnp.tile(cf, (tm // cf.shape[0], 1))
        ss = jnp.tile(ss, (tm // ss.shape[0], 1))
    return cf, ss


def _out_kernel(*refs, gated, final):
    refs = list(refs)
    o_ref = refs.pop(0)
    g_ref = refs.pop(0) if gated else None
    x_ref, w_ref = refs.pop(0), refs.pop(0)
    nf_ref = refs.pop(0) if final else None
    y_ref = refs.pop(0)
    o = o_ref[...]
    if gated:
        o = (o.astype(F32) * _silu(g_ref[...].astype(F32))).astype(BF16)
    y = x_ref[...] + _dot(o, w_ref[...])
    if final:
        ms = jnp.mean(y * y, axis=-1, keepdims=True)
        y = y * lax.rsqrt(ms + EPS) * nf_ref[...]
    y_ref[...] = y


def _out_proj(o, g, x2d, w_bf16, norm_f, tm):
    T = x2d.shape[0]
    gated, final = g is not None, norm_f is not None
    args, specs = [o], [_row_spec(tm, BRANCH)]
    if gated:
        args.append(g)
        specs.append(_row_spec(tm, BRANCH))
    args += [x2d, w_bf16]
    specs += [_row_spec(tm, D_MODEL), _full_spec(w_bf16.shape)]
    if final:
        args.append(norm_f.reshape(1, D_MODEL))
        specs.append(_full_spec((1, D_MODEL)))
    return pl.pallas_call(
        functools.partial(_out_kernel, gated=gated, final=final),
        grid=(T // tm,), in_specs=specs, out_specs=_row_spec(tm, D_MODEL),
        out_shape=jax.ShapeDtypeStruct((T, D_MODEL), F32),
        compiler_params=_params("parallel"),
    )(*args)


def _cumsum_kernel(x_ref, o_ref, *, n_blk):
    tril = _tril(PAGE).astype(F32)
    carry = jnp.zeros((1, N_HEADS), F32)
    for c in range(n_blk):
        y = _dot(tril, x_ref[0, c * PAGE:(c + 1) * PAGE, :], HIGHEST) + carry
        o_ref[0, c * PAGE:(c + 1) * PAGE, :] = y
        carry = y[PAGE - 1:PAGE, :]


def _cumsum_seq(logf):
    B, L, H = logf.shape
    return pl.pallas_call(
        functools.partial(_cumsum_kernel, n_blk=L // PAGE),
        grid=(B,),
        in_specs=[pl.BlockSpec((1, L, H), lambda b: (b, 0, 0))],
        out_specs=pl.BlockSpec((1, L, H), lambda b: (b, 0, 0)),
        out_shape=jax.ShapeDtypeStruct((B, L, H), F32),
        compiler_params=_params("parallel"),
    )(logf)


def _softmax_step(s, m, l, acc, pv_fn):
    m_new = jnp.maximum(m, jnp.max(s, axis=-1, keepdims=True))
    alpha = jnp.exp(m - m_new)
    p = jnp.exp(s - m_new)
    l = alpha * l + jnp.sum(p, axis=-1, keepdims=True)
    acc = alpha * acc + pv_fn(p.astype(BF16))
    return m_new, l, acc


def _fox_attn_kernel(q_ref, k_ref, v_ref, ck_ref, o_ref, *, tq, tk, ts):
    i = pl.program_id(2)
    n_sub = tq // ts
    qs = [q_ref[0, r * ts:(r + 1) * ts, :] for r in range(n_sub)]

    def step(j, carry, masked):
        start = pl.multiple_of(j * tk, tk)
        k = k_ref[0, pl.ds(start, tk), :]
        v = v_ref[0, pl.ds(start, tk), :]
        ck = ck_ref[0, 0, j]
        out = []
        for r in range(n_sub):
            s = _dot_nt(qs[r], k) - ck
            if masked:
                s = jnp.where(start + _iota((ts, tk), 1) <= i * tq + r * ts + _iota((ts, tk), 0), s, NEG)
            out.append(_softmax_step(s, *carry[r], lambda p: _dot(p, v)))
        return tuple(out)

    carry = tuple((jnp.full((ts, 1), NEG, F32), jnp.zeros((ts, 1), F32), jnp.zeros((ts, HEAD_DIM), F32))
                  for _ in range(n_sub))
    n_diag = tq // tk
    carry = lax.fori_loop(0, i * n_diag, lambda j, c: step(j, c, False), carry)
    for d in range(n_diag):
        carry = step(i * n_diag + d, carry, True)
    for r, (_, l, acc) in enumerate(carry):
        o_ref[0, r * ts:(r + 1) * ts, :] = (acc / l).astype(BF16)


def _fox_attn_prompt(q, k, v, cum, tq, tk, ts):
    B, L, _ = q.shape
    nb = L // tq
    ck = jnp.swapaxes(cum, 1, 2).reshape(B, N_HEADS, L // tk, 1, tk)
    return pl.pallas_call(
        functools.partial(_fox_attn_kernel, tq=tq, tk=tk, ts=ts),
        grid=(B, N_HEADS, nb),
        in_specs=[
            pl.BlockSpec((1, tq, HEAD_DIM), lambda b, h, i: (b, i, h)),
            pl.BlockSpec((1, L, HEAD_DIM), lambda b, h, i: (b, 0, h)),
            pl.BlockSpec((1, L, HEAD_DIM), lambda b, h, i: (b, 0, h)),
            pl.BlockSpec((1, 1, L // tk, 1, tk), lambda b, h, i: (b, h, 0, 0, 0)),
        ],
        out_specs=pl.BlockSpec((1, tq, HEAD_DIM), lambda b, h, i: (b, i, h)),
        out_shape=jax.ShapeDtypeStruct((B, L, BRANCH), BF16),
        compiler_params=_params("parallel", "parallel", "arbitrary"),
    )(q, k, v, ck)


def _cumsum_lanes(x):
    lane = _iota(x.shape, 1)
    d = 1
    while d < x.shape[1]:
        x = x + jnp.where(lane >= d, pltpu.roll(x, d, 1), 0.0)
        d *= 2
    return x


def _fox_dec_kernel(pt_ref, q_ref, kn_ref, vn_ref, lfn_ref, *rest, pps, n_steps, dl):
    k_refs, v_refs, lf_refs = rest[0:pps], rest[pps:2 * pps], rest[2 * pps:3 * pps]
    o_ref = rest[3 * pps]
    m_ref, l_ref, acc_ref, car_ref = rest[3 * pps + 1:]
    p = pl.program_id(1)
    nr = N_HEADS * dl
    n_chain = m_ref.shape[0]

    @pl.when(p == 0)
    def _():
        m_ref[...] = jnp.full(m_ref.shape, NEG, F32)
        l_ref[...] = jnp.zeros(l_ref.shape, F32)
        acc_ref[...] = jnp.zeros(acc_ref.shape, F32)
        car_ref[...] = jnp.zeros((N_HEADS, LANES), F32)

    qs = [q_ref[0, :, h * HEAD_DIM:(h + 1) * HEAD_DIM] for h in range(N_HEADS)]

    def attend(chains):
        off, cums = car_ref[...], {}
        for ci, (_, blocks, n_tok, _) in enumerate(chains):
            for bi, (_, _, lf) in enumerate(blocks):
                c = _cumsum_lanes(lf)
                cums[ci, bi] = (c + off)[:, 0:n_tok]
                off = off + jnp.broadcast_to(c[:, LANES - 1:LANES], (N_HEADS, LANES))
        car_ref[...] = off

        def head_rows(blocks, n_tok, which, h):
            return jnp.concatenate([blk[which][_head_rows(h, n_tok, N_HEADS), :].astype(BF16) for blk in blocks],
                                   axis=0)

        ss = []
        for ci, (_, blocks, n_tok, mask) in enumerate(chains):
            cum = jnp.concatenate([cums[ci, bi] for bi in range(len(blocks))], axis=1)
            s = jnp.concatenate([_dot_nt(qs[h], head_rows(blocks, n_tok, 0, h)) - cum[h:h + 1, :]
                                 for h in range(N_HEADS)], axis=0)
            ss.append(s if mask is None else jnp.where(mask, s, NEG))
        for (st, blocks, n_tok, _), s in zip(chains, ss):
            def pv(pb, blocks=blocks, n_tok=n_tok):
                return jnp.concatenate([_dot(pb[h * dl:(h + 1) * dl], head_rows(blocks, n_tok, 1, h))
                                        for h in range(N_HEADS)], axis=0)
            m_ref[st], l_ref[st], acc_ref[st] = _softmax_step(s, m_ref[st], l_ref[st], acc_ref[st], pv)

    n_page_chain = n_chain - 1
    per_chain = pps // n_page_chain
    page_chains = [(ci, [(k_refs[r], v_refs[r], lf_refs[r][...]) for r in range(ci * per_chain, (ci + 1) * per_chain)],
                    PAGE, None) for ci in range(n_page_chain)]
    causal = _iota((nr, dl), 1) <= _iota((nr, dl), 0) % dl
    new_chain = (n_page_chain, [(kn_ref, vn_ref, lfn_ref[0])], dl, causal)

    def finish():
        m = m_ref[0]
        for ci in range(1, n_chain):
            m = jnp.maximum(m, m_ref[ci])
        l, acc = jnp.zeros((nr, 1), F32), jnp.zeros((nr, HEAD_DIM), F32)
        for ci in range(n_chain):
            w = jnp.exp(m_ref[ci] - m)
            l, acc = l + w * l_ref[ci], acc + w * acc_ref[ci]
        out = acc / l
        for h in range(N_HEADS):
            o_ref[0, :, h * HEAD_DIM:(h + 1) * HEAD_DIM] = out[h * dl:(h + 1) * dl].astype(BF16)

    if n_steps == 1:
        attend(page_chains + [new_chain])
        finish()
    else:
        attend(page_chains)

        @pl.when(p == n_steps - 1)
        def _():
            attend([new_chain])
            finish()


def _fox_attn_decode(q, kn, vn, lfn, cache_k, cache_v, cache_lf, layer, pt_flat, n_pages, pps):
    DB, DL, _ = q.shape
    n_steps = n_pages // pps
    nr = N_HEADS * DL
    n_chain = max(pps // 4, 1) + 1
    lfn_t =jnp.pad(jnp.swapaxes(lfn, 1, 2), ((0, 0), (0, 0), (0, LANES - DL)))

    def page_spec(rows, r):
        return pl.BlockSpec((None, None, rows, LANES),
                            lambda b, p, pt, r=r: (layer, pt[b * n_pages + p * pps + r], 0, 0))

    new_rows = pl.BlockSpec((DL * N_HEADS, HEAD_DIM), lambda b, p, pt: (b, 0))
    tok = lambda n, w: pl.BlockSpec((1, n, w), lambda b, p, pt: (b, 0, 0))
    in_specs = [tok(DL, BRANCH), new_rows, new_rows, tok(N_HEADS, LANES)]
    in_specs += [page_spec(PAGE * N_HEADS, r) for r in range(pps)] * 2 + [page_spec(N_HEADS, r) for r in range(pps)]
    grid_spec = pltpu.PrefetchScalarGridSpec(
        num_scalar_prefetch=1, grid=(DB, n_steps), in_specs=in_specs, out_specs=tok(DL, BRANCH),
        scratch_shapes=[pltpu.VMEM((n_chain, nr, 1), F32), pltpu.VMEM((n_chain, nr, 1), F32),
                        pltpu.VMEM((n_chain, nr, HEAD_DIM), F32), pltpu.VMEM((N_HEADS, LANES), F32)])
    return pl.pallas_call(
        functools.partial(_fox_dec_kernel, pps=pps, n_steps=n_steps, dl=DL),
        grid_spec=grid_spec,
        out_shape=jax.ShapeDtypeStruct((DB, DL, BRANCH), BF16),
        compiler_params=_params("parallel", "arbitrary"),
    )(pt_flat, q, kn, vn, lfn_t, *([cache_k] * pps), *([cache_v] * pps), *([cache_lf] * pps))


def _key_to_float(t):
    return pltpu.bitcast(t ^ ((t >> 31) & 0x7FFFFFFF), F32)


def _kth_largest(count_ge, shape, n_sel, bits_per_step):
    def step(si, t):
        shift = 32 - bits_per_step * (si + 1)
        best = t
        for j in range(1, 2 ** bits_per_step):
            cand = t ^ lax.shift_left(jnp.int32(j), shift)
            best = jnp.where(count_ge(_key_to_float(cand)) >= n_sel, jnp.maximum(best, cand), best)
        return best
    t = lax.fori_loop(0, 32 // bits_per_step, step, jnp.full(shape, INT_MIN, jnp.int32))
    return _key_to_float(jnp.maximum(t, NEG_INF_KEY + 1))


def _strict_upper(n):
    return (_iota((n, n), 0) < _iota((n, n), 1)).astype(BF16)


def _dsa_attn_kernel(q_ref, qi_ref, wi_ref, k_ref, v_ref, ki_ref, o_ref, sc_ref, sct_ref, wb_ref, *, n_sel, tq, W):
    i = pl.program_id(1)
    nch = (i * tq + tq + W - 1) // W
    reps = W // LANES
    wi = wi_ref[0]
    for h in range(IDX_HEADS):
        wb_ref[h] = jnp.broadcast_to(wi[:, h:h + 1], (tq, LANES))
    row, col = _iota((tq, W), 0), _iota((tq, W), 1)

    def fill(c, _):
        kc = ki_ref[0, pl.ds(pl.multiple_of(c * W, W), W), :].astype(BF16)
        sc = jnp.zeros((tq, W), F32)
        for h in range(IDX_HEADS):
            d = _dot_nt(qi_ref[0, :, h * IDX_DIM:(h + 1) * IDX_DIM], kc)
            sc = sc + jnp.concatenate([wb_ref[h]] * reps, axis=1) * jnp.maximum(d, 0.0)
        sc = jnp.where(c * W + col <= i * tq + row, sc, -jnp.inf)
        sc_ref[c] = sc
        sct_ref[c] = sc.T
        return 0

    lax.fori_loop(0, nch, fill, 0)

    def count(pred):
        def body(c, acc):
            x = jnp.where(pred(sc_ref[c]), 1.0, 0.0)
            for r in range(reps):
                acc = acc + x[:, r * LANES:(r + 1) * LANES]
            return acc
        return jnp.sum(lax.fori_loop(0, nch, body, jnp.zeros((tq, LANES), F32)), axis=-1, keepdims=True)

    def count_ge_t(f):
        f8 = jnp.broadcast_to(f, (SUBLANES, tq))
        n_acc = 4

        def body(c, accs):
            accs = list(accs)
            for r in range(W // SUBLANES):
                x = jnp.where(sct_ref[c, r * SUBLANES:(r + 1) * SUBLANES, :] >= f8, 1.0, 0.0)
                accs[r % n_acc] = accs[r % n_acc] + x
            return tuple(accs)

        accs = lax.fori_loop(0, nch, body, tuple(jnp.zeros((SUBLANES, tq), F32) for _ in range(n_acc)))
        return jnp.sum((accs[0] + accs[1]) + (accs[2] + accs[3]), axis=0, keepdims=True)

    thr_row = _kth_largest(count_ge_t, (1, tq), n_sel, 1)
    eye = _iota((tq, tq), 0) == _iota((tq, tq), 1)
    thr = jnp.sum(jnp.where(eye, jnp.broadcast_to(thr_row, (tq, tq)), 0.0), axis=-1, keepdims=True)

    @pl.when(jnp.max(count(lambda s: s >= thr)) > n_sel)
    def _():
        need = n_sel - count(lambda s: s > thr)
        upper = _strict_upper(W)

        def body(c, seen):
            sc = sc_ref[c]
            eq = sc == thr
            eqf = jnp.where(eq, 1.0, 0.0)
            rank = seen + _dot(eqf.astype(BF16), upper)
            sc_ref[c] = jnp.where(eq & (rank >= need), -jnp.inf, sc)
            return seen + jnp.sum(eqf, axis=-1, keepdims=True)

        lax.fori_loop(0, nch, body, jnp.zeros((tq, 1), F32))

    qgs = [jnp.concatenate(
        [q_ref[0, :, (g * DSA_GROUP + r) * HEAD_DIM:(g * DSA_GROUP + r + 1) * HEAD_DIM] for r in range(DSA_GROUP)],
        axis=0) for g in range(DSA_KV_HEADS)]

    def body(c, carry):
        start = pl.multiple_of(c * W, W)
        bias = jnp.where(sc_ref[c] >= thr, 0.0, NEG)[None]
        out = []
        for g in range(DSA_KV_HEADS):
            m, l, acc = carry[g]
            gsl = slice(g * HEAD_DIM, (g + 1) * HEAD_DIM)
            s = _dot_nt(qgs[g], k_ref[0, pl.ds(start, W), gsl]).reshape(DSA_GROUP, tq, W) + bias
            m_new = jnp.maximum(m, jnp.max(s, axis=-1, keepdims=True))
            alpha = jnp.exp(m - m_new)
            p = jnp.exp(s - m_new)
            l = alpha * l + jnp.sum(p, axis=-1, keepdims=True)
            pv = _dot(p.reshape(DSA_GROUP * tq, W).astype(BF16), v_ref[0, pl.ds(start, W), gsl])
            out.append((m_new, l, alpha * acc + pv.reshape(DSA_GROUP, tq, HEAD_DIM)))
        return tuple(out)

    init = tuple((jnp.full((DSA_GROUP, tq, 1), NEG, F32), jnp.zeros((DSA_GROUP, tq, 1), F32),
                  jnp.zeros((DSA_GROUP, tq, HEAD_DIM), F32)) for _ in range(DSA_KV_HEADS))
    res = lax.fori_loop(0, nch, body, init)
    for g, (_, l, acc) in enumerate(res):
        out = acc / l
        for r in range(DSA_GROUP):
            hh = g * DSA_GROUP + r
            o_ref[0, :, hh * HEAD_DIM:(hh + 1) * HEAD_DIM] = out[r].astype(BF16)


def _dsa_attn_prompt(q, qi, wi, k, v, ki, n_sel, tq):
    B, L, _ = q.shape
    nb = L // tq
    W = min(4 * LANES, L)
    blk = lambda w: pl.BlockSpec((1, tq, w), lambda b, i: (b, i, 0))
    full = lambda w: pl.BlockSpec((1, L, w), lambda b, i: (b, 0, 0))
    return pl.pallas_call(
        functools.partial(_dsa_attn_kernel, n_sel=n_sel, tq=tq, W=W),
        grid=(B, nb),
        in_specs=[blk(BRANCH), blk(IDX_HEADS * IDX_DIM), blk(IDX_HEADS), full(DSA_KV), full(DSA_KV), full(IDX_DIM)],
        out_specs=blk(BRANCH),
        out_shape=jax.ShapeDtypeStruct((B, L, BRANCH), BF16),
        scratch_shapes=[pltpu.VMEM((L // W, tq, W), F32), pltpu.VMEM((L // W, W, tq), F32),
                        pltpu.VMEM((IDX_HEADS, tq, LANES), F32)],
        compiler_params=_params("parallel", "arbitrary"),
    )(q, qi, wi, k, v, ki)


def _dsa_dec_kernel(pt_ref, q_ref, qi_ref, wi_ref, kn_ref, vn_ref, kin_ref, *rest, pps, n_pages, dl, n_sel, bb):
    n_pg = bb * pps
    k_refs, v_refs, ki_refs = rest[0:n_pg], rest[n_pg:2 * n_pg], rest[2 * n_pg:3 * n_pg]
    o_ref = rest[3 * n_pg]
    ks_ref, vs_ref, kis_ref = rest[3 * n_pg + 1:]
    p = pl.program_id(1)
    past = n_pages * PAGE
    nk = past + PAGE
    nr = bb * dl
    G = DSA_KV_HEADS

    for e in range(bb):
        for r in range(pps):
            page = p * pps + r
            start = pl.multiple_of(page * (PAGE * G), PAGE * G)
            ks_ref[e, pl.ds(start, PAGE * G), :] = k_refs[e * pps + r][...]
            vs_ref[e, pl.ds(start, PAGE * G), :] = v_refs[e * pps + r][...]
            kis_ref[e, page] = ki_refs[e * pps + r][...]

    @pl.when(p == n_pages // pps - 1)
    def _():
        for e in range(bb):
            for ref, new in ((ks_ref, kn_ref), (vs_ref, vn_ref)):
                ref[e, past * G:(past + dl) * G, :] = new[e * dl * G:(e + 1) * dl * G, :]
                ref[e, (past + dl) * G:nk * G, :] = jnp.zeros(((PAGE - dl) * G, HEAD_DIM), F32)
            kis_ref[e, n_pages] = kin_ref[e]

        rows = []
        for e in range(bb):
            qi, wi = qi_ref[e], wi_ref[e]
            chunks = []
            for c in range(n_pages + 1):
                wr = wi * jnp.maximum(_dot(qi, kis_ref[e, c].astype(BF16)), 0.0)
                sc = wr[0:dl]
                for h in range(1, IDX_HEADS):
                    sc = sc + wr[h * dl:(h + 1) * dl]
                chunks.append(sc)
            rows.append(jnp.concatenate(chunks, axis=1))
        score = jnp.concatenate(rows, axis=0)
        valid = _iota((nr, nk), 1) <= past + _iota((nr, nk), 0) % dl
        score = jnp.where(valid, score, -jnp.inf)

        def count(mask):
            return jnp.sum(jnp.where(mask, 1.0, 0.0), axis=-1, keepdims=True)

        thr = _kth_largest(lambda f: count(score >= f), (nr, 1), n_sel, 2)

        def drop_surplus_ties(sc):
            need = n_sel - count(sc > thr)
            upper = _strict_upper(PAGE)
            seen, kept = jnp.zeros((nr, 1), F32), []
            for c in range(n_pages + 1):
                scc = sc[:, c * PAGE:(c + 1) * PAGE]
                eq = scc == thr
                eqf = jnp.where(eq, 1.0, 0.0)
                rank = seen + _dot(eqf.astype(BF16), upper)
                kept.append(jnp.where(eq & (rank >= need), -jnp.inf, scc))
                seen = seen + jnp.sum(eqf, axis=-1, keepdims=True)
            return jnp.concatenate(kept, axis=1)

        score = lax.cond(jnp.max(count(score >= thr)) > n_sel, drop_surplus_ties, lambda sc: sc, score)
        sel = score >= thr

        units = [(e, g) for e in range(bb) for g in range(G)]

        def stage(fn):
            return [fn(i, e, g) for i, (e, g) in enumerate(units)]

        qg = stage(lambda i, e, g: jnp.concatenate(
            [q_ref[e, :, (g * DSA_GROUP + r) * HEAD_DIM:(g * DSA_GROUP + r + 1) * HEAD_DIM].astype(F32)
             for r in range(DSA_GROUP)], axis=0).astype(BF16))
        s = stage(lambda i, e, g: jnp.where(
            sel[e * dl:(e + 1) * dl][None],
            _dot_nt(qg[i], ks_ref[e, _head_rows(g, nk, G), :].astype(BF16)).reshape(DSA_GROUP, dl, nk), NEG))
        pr = stage(lambda i, e, g: jnp.exp(s[i] - jnp.max(s[i], axis=-1, keepdims=True)))
        den = stage(lambda i, e, g: jnp.sum(pr[i], axis=-1, keepdims=True))
        pv = stage(lambda i, e, g: _dot(pr[i].reshape(DSA_GROUP * dl, nk).astype(BF16),
                                        vs_ref[e, _head_rows(g, nk, G), :].astype(BF16)))
        for i, (e, g) in enumerate(units):
            out = pv[i].reshape(DSA_GROUP, dl, HEAD_DIM) / den[i]
            for r in range(DSA_GROUP):
                hh = g * DSA_GROUP + r
                o_ref[e, :, hh * HEAD_DIM:(hh + 1) * HEAD_DIM] = out[r].astype(BF16)


def _dsa_attn_decode(q, qi, wi, kn, vn, kin, cache_k, cache_v, cache_ki, layer, pt_flat, n_pages, pps, n_sel):
    DB, DL, _ = q.shape
    nk = n_pages * PAGE + PAGE
    G = DSA_KV_HEADS
    bb = max(d for d in (1, 2, 4) if DB % d == 0)
    qi_hq = jnp.swapaxes(qi.reshape(DB, DL, IDX_HEADS, IDX_DIM), 1, 2).reshape(DB, IDX_HEADS * DL, IDX_DIM)
    wi_hq = jnp.swapaxes(wi.reshape(DB, DL, IDX_HEADS), 1, 2).reshape(DB, IDX_HEADS * DL, 1)
    kin_t = jnp.pad(jnp.swapaxes(kin, 1, 2), ((0, 0), (0, 0), (0, PAGE - DL)))

    def page_specs(rows):
        return [pl.BlockSpec((None, None, rows, LANES),
                             lambda b, p, pt, e=e, r=r: (layer, pt[(b * bb + e) * n_pages + p * pps + r], 0, 0))
                for e in range(bb) for r in range(pps)]

    tok = lambda n, w: pl.BlockSpec((bb, n, w), lambda b, p, pt: (b, 0, 0))
    new_rows = pl.BlockSpec((bb * DL * G, HEAD_DIM), lambda b, p, pt: (b, 0))
    in_specs = [tok(DL, BRANCH), tok(IDX_HEADS * DL, IDX_DIM), tok(IDX_HEADS * DL, 1),
                new_rows, new_rows, tok(IDX_DIM, PAGE)]
    in_specs += page_specs(PAGE * G) * 2 + page_specs(IDX_DIM)
    n_pg = bb * pps
    grid_spec = pltpu.PrefetchScalarGridSpec(
        num_scalar_prefetch=1, grid=(DB // bb, n_pages // pps), in_specs=in_specs, out_specs=tok(DL, BRANCH),
        scratch_shapes=[pltpu.VMEM((bb, nk * G, HEAD_DIM), F32), pltpu.VMEM((bb, nk * G, HEAD_DIM), F32),
                        pltpu.VMEM((bb, n_pages + 1, IDX_DIM, PAGE), F32)])
    return pl.pallas_call(
        functools.partial(_dsa_dec_kernel, pps=pps, n_pages=n_pages, dl=DL, n_sel=n_sel, bb=bb),
        grid_spec=grid_spec,
        out_shape=jax.ShapeDtypeStruct((DB, DL, BRANCH), BF16),
        compiler_params=_params("parallel", "arbitrary"),
    )(pt_flat, q, qi_hq, wi_hq, kn, vn, kin_t, *([cache_k] * n_pg), *([cache_v] * n_pg), *([cache_ki] * n_pg))


def _gdn_kernel(x_ref, z_ref, g_ref, b_ref, cw_ref, nw_ref, cs_ref, s0_ref, o_ref, sout_ref, xb_ref, s_ref,
                *, C, nc, bb):
    c = pl.program_id(1)
    HIST = SUBLANES

    @pl.when(c == 0)
    def _():
        xb_ref[:, 0:HIST, :] = cs_ref[...]
        s_ref[...] = s0_ref[...]

    incl, strict = _tril(C), _tril(C, strict=True)
    eye_c = (_iota((C, C), 0) == _iota((C, C), 1)).astype(F32)
    eye_h = (_iota((N_HEADS, N_HEADS), 0) == _iota((N_HEADS, N_HEADS), 1)).astype(F32)
    n_dbl = int(math.log2(C)) - 1

    convs, G_alls, Gt_alls = [], [], []
    for e in range(bb):
        xb_ref[e, HIST:HIST + C, :] = x_ref[e]
        conv = xb_ref[e, HIST:HIST + C, :] * cw_ref[CONV_W - 1:CONV_W, :]
        for j in range(CONV_W - 1):
            off = HIST - (CONV_W - 1) + j
            conv = conv + xb_ref[e, off:off + C, :] * cw_ref[j:j + 1, :]
        hist = xb_ref[e, C:C + HIST, :]
        xb_ref[e, 0:HIST, :] = hist
        convs.append(_silu(conv))
        G = _dot(incl.astype(F32), g_ref[e], HIGHEST)
        G_alls.append(G)
        Gt_alls.append(_dot_nt(eye_h, G, HIGHEST))

    units = [(e, h) for e in range(bb) for h in range(N_HEADS)]

    def stage(fn):
        return [fn(i, e, h) for i, (e, h) in enumerate(units)]

    def head(e, j, h):
        return convs[e][:, j * BRANCH + h * HEAD_DIM:j * BRANCH + (h + 1) * HEAD_DIM]

    def l2n(a):
        return a * lax.rsqrt(jnp.sum(a * a, axis=-1, keepdims=True) + EPS)

    q = stage(lambda i, e, h: l2n(head(e, 0, h)) * ATTN_SCALE)
    k = stage(lambda i, e, h: l2n(head(e, 1, h)))
    v = stage(lambda i, e, h: head(e, 2, h))
    Gc = stage(lambda i, e, h: G_alls[e][:, h:h + 1])
    bc = stage(lambda i, e, h: b_ref[e][:, h:h + 1])
    decay = stage(lambda i, e, h: jnp.exp(jnp.where(incl, Gc[i] - Gt_alls[e][h:h + 1, :], -jnp.inf)))
    kqk = stage(lambda i, e, h: _dot1(jnp.concatenate([k[i], q[i]], axis=0), k[i], _dot_nt))
    A = stage(lambda i, e, h: jnp.where(strict, bc[i] * kqk[i][0:C] * decay[i], 0.0))
    T = stage(lambda i, e, h: eye_c - A[i])
    P = stage(lambda i, e, h: _dot1(A[i], A[i]))
    for it in range(n_dbl):
        TP = stage(lambda i, e, h: _dot1(T[i], P[i]))
        T = stage(lambda i, e, h: T[i] + TP[i])
        if it + 1 < n_dbl:
            P = stage(lambda i, e, h: _dot1(P[i], P[i]))
    eG = stage(lambda i, e, h: jnp.exp(Gc[i]))
    WU = stage(lambda i, e, h: _dot1(T[i], jnp.concatenate([k[i] * (bc[i] * eG[i]), v[i] * bc[i]], axis=1)))
    Aqk = stage(lambda i, e, h: kqk[i][C:2 * C] * decay[i])
    G_last = stage(lambda i, e, h: Gc[i][C - 1:C, :])
    k_dec = stage(lambda i, e, h: k[i] * jnp.exp(G_last[i] - Gc[i]))
    S = stage(lambda i, e, h: s_ref[e, h])
    WqS = stage(lambda i, e, h: _dot1(jnp.concatenate([WU[i][:, 0:HEAD_DIM], q[i]], axis=0), S[i]))
    Vn = stage(lambda i, e, h: WU[i][:, HEAD_DIM:2 * HEAD_DIM] - WqS[i][0:C])
    o = stage(lambda i, e, h: eG[i] * WqS[i][C:2 * C] + _dot1(Aqk[i], Vn[i]))
    S_new = stage(lambda i, e, h: jnp.exp(G_last[i]) * S[i] + _dot1(k_dec[i], Vn[i], _dot_tn))
    for i, (e, h) in enumerate(units):
        sl = slice(h * HEAD_DIM, (h + 1) * HEAD_DIM)
        s_ref[e, h] = S_new[i]
        on = o[i] * lax.rsqrt(jnp.mean(o[i] * o[i], axis=-1, keepdims=True) + EPS) * nw_ref[...]
        o_ref[e, :, sl] = (on * _silu(z_ref[e, :, sl].astype(F32))).astype(BF16)

    @pl.when(c == nc - 1)
    def _():
        sout_ref[...] = s_ref[...]


def _gdn_mix(qkv, z, g, beta, conv_w, norm_w, conv_state, S0):
    B, L, _ = qkv.shape
    C = min(GDN_CHUNK, L)
    nc = L // C
    assert C & (C - 1) == 0 and L % C == 0
    bb = max(d for d in (1, 2, 4) if B % d == 0 and d * C <= 2 * GDN_CHUNK)
    cs = jnp.pad(conv_state, ((0, 0), (SUBLANES - (CONV_W - 1), 0), (0, 0)))
    blk = lambda w: pl.BlockSpec((bb, C, w), lambda b, c: (b, c, 0))
    state = pl.BlockSpec((bb, N_HEADS, HEAD_DIM, HEAD_DIM), lambda b, c: (b, 0, 0, 0))
    return pl.pallas_call(
        functools.partial(_gdn_kernel, C=C, nc=nc, bb=bb),
        grid=(B // bb, nc),
        in_specs=[blk(3 * BRANCH), blk(BRANCH), blk(N_HEADS), blk(N_HEADS),
                  pl.BlockSpec((CONV_W, 3 * BRANCH), lambda b, c: (0, 0)),
                  pl.BlockSpec((1, HEAD_DIM), lambda b, c: (0, 0)),
                  pl.BlockSpec((bb, SUBLANES, 3 * BRANCH), lambda b, c: (b, 0, 0)),
                  state],
        out_specs=[blk(BRANCH), state],
        out_shape=[jax.ShapeDtypeStruct((B, L, BRANCH), BF16),
                   jax.ShapeDtypeStruct((B, N_HEADS, HEAD_DIM, HEAD_DIM), F32)],
        scratch_shapes=[pltpu.VMEM((bb, C + SUBLANES, 3 * BRANCH), F32),
                        pltpu.VMEM((bb, N_HEADS, HEAD_DIM, HEAD_DIM), F32)],
        compiler_params=_params("parallel", "arbitrary"),
    )(qkv, z, g, beta, conv_w, norm_w.reshape(1, HEAD_DIM), cs, S0)


def _pad_cols(w, n=LANES):
    return jnp.pad(w, ((0, 0), (0, n - w.shape[1])))


def _row_tile(T):
    for tm in (512, 256):
        if T % tm == 0 and T >= 4 * tm:
            return tm
    return T


def kernel(x_prompt, x_sample, cache_fox_k, cache_fox_v, cache_fox_logf, cache_dsa_k, cache_dsa_v, cache_dsa_kidx, state_gdn_conv, state_gdn_S, page_table, norm_w, norm_f, fox_w_in, fox_b_f, fox_w_out, dsa_w_in, dsa_w_out, gdn_w_in, gdn_conv_w, gdn_A_log, gdn_dt_bias, gdn_norm_w, gdn_w_out):
    B, L, _ = x_prompt.shape
    DB, DL, _ = x_sample.shape
    n_pages = page_table.shape[1]
    past = n_pages * PAGE
    depth = norm_w.shape[0]
    pt_flat = page_table.reshape(-1)
    pages_per_step = lambda want: max(p for p in (1, 2, 4, 8, 16) if p <= want and n_pages % p == 0)
    fox_pps, dsa_pps = pages_per_step(16), pages_per_step(4)
    groups = ((B, L), (DB, DL))
    xs = [x_prompt.reshape(B * L, D_MODEL), x_sample.reshape(DB * DL, D_MODEL)]
    tms = [_row_tile(nb * nl) for nb, nl in groups]
    pos = [jnp.arange(L), past + jnp.arange(DL)]
    tq_p = max(t for t in (PAGE, 2 * PAGE, 4 * PAGE) if L % t == 0)

    pool = cache_fox_k.shape[1]
    fox_ck = cache_fox_k.reshape(-1, pool, PAGE * N_HEADS, HEAD_DIM)
    fox_cv = cache_fox_v.reshape(-1, pool, PAGE * N_HEADS, HEAD_DIM)
    fox_clf = jnp.swapaxes(cache_fox_logf, 2, 3)
    dsa_ck = cache_dsa_k.reshape(-1, pool, PAGE * DSA_KV_HEADS, HEAD_DIM)
    dsa_cv = cache_dsa_v.reshape(-1, pool, PAGE * DSA_KV_HEADS, HEAD_DIM)
    dsa_cki = jnp.swapaxes(cache_dsa_kidx, 2, 3)

    st = {name: ([], []) for name in ("fox_k", "fox_v", "fox_lf", "dsa_k", "dsa_v", "dsa_ki", "gdn_conv", "gdn_S")}
    finals = [None, None]
    n_fox = fox_w_in.shape[0]
    fox_kv_stack = [None, None]

    for i in range(depth):
        kind, j = i % 3, i // 3
        nw = norm_w[i].reshape(1, D_MODEL)
        last = i == depth - 1
        for gi, (nb, nl) in enumerate(groups):
            x2d, tm = xs[gi], tms[gi]
            T = nb * nl
            prompt = gi == 0
            sh = lambda a: a.reshape(nb, nl, a.shape[-1])
            if kind == 0:
                w = fox_w_in[j]
                w_main = jnp.concatenate([w[:, :3 * BRANCH], w[:, 3 * BRANCH + N_HEADS:]], axis=1).astype(BF16)
                w_small = _pad_cols(w[:, 3 * BRANCH:3 * BRANCH + N_HEADS]).astype(BF16)
                outs = [(1, BRANCH, BF16), (N_HEADS, HEAD_DIM, F32), (N_HEADS, HEAD_DIM, F32), (1, BRANCH, BF16),
                        (1, N_HEADS, F32)]
                stacked = None
                if prompt:
                    outs += [(1, BRANCH, BF16), (1, BRANCH, BF16)]
                    stacked = (j, n_fox, {1: fox_kv_stack[0], 2: fox_kv_stack[1]})
                res = _in_proj(_fox_in_kernel, x2d, [nw, w_main, w_small, fox_b_f[j].reshape(1, N_HEADS)], [], outs,
                               tm, stacked)
                q, k, v, g, lf = res[:5]
                if prompt:
                    o = _fox_attn_prompt(sh(q), sh(res[5]), sh(res[6]), _cumsum_seq(sh(lf)), tq_p, min(tq_p, 4 * PAGE), tq_p)
                    fox_kv_stack = [k, v]
                else:
                    o = _fox_attn_decode(sh(q), k, v, sh(lf), fox_ck, fox_cv, fox_clf, j, pt_flat, n_pages, fox_pps)
                    st["fox_k"][gi].append(k.reshape(nb, nl, N_HEADS, HEAD_DIM))
                    st["fox_v"][gi].append(v.reshape(nb, nl, N_HEADS, HEAD_DIM))
                st["fox_lf"][gi].append(lf.reshape(nb, nl, N_HEADS))
                w_out = fox_w_out[j]
            elif kind == 1:
                w = dsa_w_in[j]
                o_qi = BRANCH + 2 * DSA_KV
                o_wi = o_qi + IDX_HEADS * IDX_DIM
                o_ki = o_wi + IDX_HEADS
                o_g = o_ki + IDX_DIM
                w_main = jnp.concatenate([w[:, :o_wi], w[:, o_g:]], axis=1).astype(BF16)
                w_small = _pad_cols(jnp.concatenate([w[:, o_ki:o_g], w[:, o_wi:o_ki]], axis=1)).astype(BF16)
                tables = list(_rope_tables(pos[gi], HEAD_DIM, tm) + _rope_tables(pos[gi], IDX_DIM, tm))
                outs = [(1, BRANCH, BF16), (DSA_KV_HEADS, HEAD_DIM, F32), (DSA_KV_HEADS, HEAD_DIM, F32),
                        (1, IDX_HEADS * IDX_DIM, BF16), (1, BRANCH, BF16), (1, IDX_DIM, F32), (1, IDX_HEADS, F32)]
                if prompt:
                    outs += [(1, DSA_KV, BF16), (1, DSA_KV, BF16)]
                res = _in_proj(_dsa_in_kernel, x2d, [nw, w_main, w_small], tables, outs, tm)
                q, k, v, qi, g, ki, wi = res[:7]
                if prompt:
                    o = _dsa_attn_prompt(sh(q), sh(qi), sh(wi), sh(res[7]), sh(res[8]), sh(ki),
                                         min(TOPK_MAX, L // 4), PAGE)
                else:
                    o = _dsa_attn_decode(sh(q), sh(qi), sh(wi), k, v, sh(ki), dsa_ck, dsa_cv, dsa_cki,
                                         j, pt_flat, n_pages, dsa_pps, min(TOPK_MAX, (past + DL) // 4))
                st["dsa_k"][gi].append(k.reshape(nb, nl, DSA_KV_HEADS, HEAD_DIM))
                st["dsa_v"][gi].append(v.reshape(nb, nl, DSA_KV_HEADS, HEAD_DIM))
                st["dsa_ki"][gi].append(ki.reshape(nb, nl, IDX_DIM))
                w_out = dsa_w_out[j]
            else:
                w = gdn_w_in[j]
                w_main = jnp.concatenate([w[:, :3 * BRANCH], w[:, 3 * BRANCH + 2 * N_HEADS:]], axis=1).astype(BF16)
                w_small = _pad_cols(w[:, 3 * BRANCH:3 * BRANCH + 2 * N_HEADS]).astype(BF16)
                qkv, z, gg, beta = _in_proj(
                    _gdn_in_kernel, x2d,
                    [nw, w_main, w_small, gdn_A_log[j].reshape(1, N_HEADS), gdn_dt_bias[j].reshape(1, N_HEADS)], [],
                    [(1, 3 * BRANCH, F32), (1, BRANCH, BF16), (1, N_HEADS, F32), (1, N_HEADS, F32)], tm)
                if prompt:
                    conv0 = jnp.zeros((nb, CONV_W - 1, 3 * BRANCH), F32)
                    S0 = jnp.zeros((nb, N_HEADS, HEAD_DIM, HEAD_DIM), F32)
                else:
                    conv0, S0 = state_gdn_conv[j], state_gdn_S[j]
                o, S_new = _gdn_mix(sh(qkv), sh(z), sh(gg), sh(beta), gdn_conv_w[j], gdn_norm_w[j], conv0, S0)
                tail = jnp.concatenate([conv0, sh(qkv)[:, -min(nl, CONV_W - 1):]], axis=1)
                st["gdn_conv"][gi].append(tail[:, -(CONV_W - 1):])
                st["gdn_S"][gi].append(S_new)
                g = None
                w_out = gdn_w_out[j]
            y = _out_proj(o.reshape(T, BRANCH), g, x2d, w_out.astype(BF16), norm_f if last else None, tm)
            if last:
                finals[gi] = y.reshape(nb, nl, D_MODEL)
            else:
                xs[gi] = y

    stk = lambda name, gi: jnp.stack(st[name][gi], axis=0)
    fox_kv_p = [a.reshape(n_fox, B, L, N_HEADS, HEAD_DIM) for a in fox_kv_stack]
    return (finals[0], finals[1],
            fox_kv_p[0], fox_kv_p[1], stk("fox_lf", 0),
            stk("fox_k", 1), stk("fox_v", 1), stk("fox_lf", 1),
            stk("dsa_k", 0), stk("dsa_v", 0), stk("dsa_ki", 0),
            stk("dsa_k", 1), stk("dsa_v", 1), stk("dsa_ki", 1),
            stk("gdn_conv", 0), stk("gdn_S", 0),
            stk("gdn_conv", 1), stk("gdn_S", 1))
```

```python
import functools
import math

import jax
import jax.numpy as jnp
from jax import lax
from jax.experimental import pallas as pl
from jax.experimental.pallas import tpu as pltpu

F32 = jnp.float32
BF16 = jnp.bfloat16
HIGHEST = lax.Precision.HIGHEST

D_MODEL = 1024
N_HEADS = 8
HEAD_DIM = 128
BRANCH = N_HEADS * HEAD_DIM
ATTN_SCALE = HEAD_DIM ** -0.5
DSA_KV_HEADS = 2
DSA_GROUP = N_HEADS // DSA_KV_HEADS
DSA_KV = DSA_KV_HEADS * HEAD_DIM
IDX_HEADS = 8
IDX_DIM = 64
IDX_SCALE = IDX_DIM ** -0.5
TOPK_MAX = 256
CONV_W = 4
GDN_CHUNK = 64
ROPE_THETA = 10000.0
EPS = 1e-6
PAGE = 128
LANES = 128
SUBLANES = 8
NEG = -1e30
INT_MIN = -(2 ** 31)
NEG_INF_KEY = INT_MIN + 0x007FFFFF
V7X_VMEM_BYTES = 64 * 1024 * 1024
VMEM_LIMIT = V7X_VMEM_BYTES * 7 // 8


def _dot(a, b, prec=None):
    return jnp.dot(a, b, preferred_element_type=F32, precision=prec)


def _dot_nt(a, b, prec=None):
    return lax.dot_general(a, b, (((1,), (1,)), ((), ())), preferred_element_type=F32, precision=prec)


def _dot_tn(a, b, prec=None):
    return lax.dot_general(a, b, (((0,), (0,)), ((), ())), preferred_element_type=F32, precision=prec)


def _dot1(a, b, dot=_dot):
    return dot(a.astype(BF16), b.astype(BF16))


def _iota(shape, axis):
    return lax.broadcasted_iota(jnp.int32, shape, axis)


def _tril(n, strict=False):
    r, c = _iota((n, n), 0), _iota((n, n), 1)
    return (r > c) if strict else (r >= c)


def _sigmoid(x):
    return 1.0 / (1.0 + jnp.exp(-x))


def _silu(x):
    return x * _sigmoid(x)


def _softplus(x):
    return jnp.maximum(x, 0.0) + jnp.log1p(jnp.exp(-jnp.abs(x)))


def _params(*sem):
    return pltpu.CompilerParams(dimension_semantics=sem, vmem_limit_bytes=VMEM_LIMIT)


def _head_rows(h, n_tok, n_heads):
    return pl.ds(h, n_tok, stride=n_heads)


def _normed(x_ref, nw_ref):
    x = x_ref[...]
    ms = jnp.mean(x * x, axis=-1, keepdims=True)
    return (x * lax.rsqrt(ms + EPS) * nw_ref[...]).astype(BF16)


def _fox_in_kernel(x_ref, nw_ref, w_ref, ws_ref, bf_ref, q_ref, k_ref, v_ref, g_ref, lf_ref, *copies):
    tm = x_ref.shape[0]
    h = _normed(x_ref, nw_ref)
    q_ref[...] = (_dot(h, w_ref[:, 0:BRANCH]) * ATTN_SCALE).astype(BF16)
    for j, ref in ((1, k_ref), (2, v_ref)):
        r = _dot(h, w_ref[:, j * BRANCH:(j + 1) * BRANCH])
        for hh in range(N_HEADS):
            ref[_head_rows(hh, tm, N_HEADS), :] = r[:, hh * HEAD_DIM:(hh + 1) * HEAD_DIM]
        if copies:
            copies[j - 1][...] = r.astype(BF16)
    g_ref[...] = _dot(h, w_ref[:, 3 * BRANCH:4 * BRANCH]).astype(BF16)
    f = _dot(h, ws_ref[...])[:, 0:N_HEADS] + bf_ref[...]
    lf_ref[...] = -_softplus(-f)


def _rope128(x, cos, sin_signed):
    return x * cos + pltpu.roll(x, HEAD_DIM // 2, 1) * sin_signed


def _rope64(x, cos, sin_signed):
    lane = _iota(x.shape, 1)
    rot = jnp.where((lane & (IDX_DIM - 1)) < IDX_DIM // 2,
                    pltpu.roll(x, LANES - IDX_DIM // 2, 1), pltpu.roll(x, IDX_DIM // 2, 1))
    return x * cos + rot * sin_signed


def _dsa_in_kernel(x_ref, nw_ref, w_ref, ws_ref, c128_ref, s128_ref, c64_ref, s64_ref,
                   q_ref, k_ref, v_ref, qi_ref, g_ref, ki_ref, wi_ref, *copies):
    tm = x_ref.shape[0]
    h = _normed(x_ref, nw_ref)
    c128, s128, c64, s64 = c128_ref[...], s128_ref[...], c64_ref[...], s64_ref[...]
    o_k, o_v, o_qi, o_g = BRANCH, BRANCH + DSA_KV, BRANCH + 2 * DSA_KV, BRANCH + 2 * DSA_KV + IDX_HEADS * IDX_DIM
    q_all = _dot(h, w_ref[:, 0:BRANCH])
    for hh in range(N_HEADS):
        sl = slice(hh * HEAD_DIM, (hh + 1) * HEAD_DIM)
        q_ref[:, sl] = (_rope128(q_all[:, sl], c128, s128) * ATTN_SCALE).astype(BF16)
    k_all = _dot(h, w_ref[:, o_k:o_k + DSA_KV])
    v_all = _dot(h, w_ref[:, o_v:o_v + DSA_KV])
    for hh in range(DSA_KV_HEADS):
        sl = slice(hh * HEAD_DIM, (hh + 1) * HEAD_DIM)
        kk = _rope128(k_all[:, sl], c128, s128)
        k_ref[_head_rows(hh, tm, DSA_KV_HEADS), :] = kk
        v_ref[_head_rows(hh, tm, DSA_KV_HEADS), :] = v_all[:, sl]
        if copies:
            copies[0][:, sl] = kk.astype(BF16)
    if copies:
        copies[1][...] = v_all.astype(BF16)
    qi_all = _dot(h, w_ref[:, o_qi:o_qi + IDX_HEADS * IDX_DIM])
    for hh in range(IDX_HEADS * IDX_DIM // LANES):
        sl = slice(hh * LANES, (hh + 1) * LANES)
        qi_ref[:, sl] = _rope64(qi_all[:, sl], c64, s64).astype(BF16)
    g_ref[...] = _dot(h, w_ref[:, o_g:o_g + BRANCH]).astype(BF16)
    small = _dot(h, ws_ref[...])
    ki_ref[...] = _rope64(small, c64, s64)[:, 0:IDX_DIM]
    wi_ref[...] = small[:, IDX_DIM:IDX_DIM + IDX_HEADS] * (IDX_HEADS ** -0.5 * IDX_SCALE)


def _gdn_in_kernel(x_ref, nw_ref, w_ref, ws_ref, alog_ref, dtb_ref, qkv_ref, z_ref, g_ref, beta_ref):
    h = _normed(x_ref, nw_ref)
    for j in range(3):
        sl = slice(j * BRANCH, (j + 1) * BRANCH)
        qkv_ref[:, sl] = _dot(h, w_ref[:, sl])
    z_ref[...] = _dot(h, w_ref[:, 3 * BRANCH:4 * BRANCH]).astype(BF16)
    small = _dot(h, ws_ref[...])
    a = small[:, 0:N_HEADS]
    b = small[:, N_HEADS:2 * N_HEADS]
    g_ref[...] = -jnp.exp(alog_ref[...]) * _softplus(a + dtb_ref[...])
    beta_ref[...] = _sigmoid(b)


def _row_spec(tm, n):
    return pl.BlockSpec((tm, n), lambda i: (i, 0))


def _full_spec(shape):
    return pl.BlockSpec(shape, lambda i: (0,) * len(shape))


def _in_proj(kern, x2d, consts, tables, outs, tm, stacked=None):
    T = x2d.shape[0]
    in_specs = [_row_spec(tm, D_MODEL)] + [_full_spec(c.shape) for c in consts[:3]]
    args = [x2d] + list(consts[:3])
    for t in tables:
        nblk = t.shape[0] // tm
        in_specs.append(pl.BlockSpec((tm, LANES), lambda i, nblk=nblk: (i % nblk, 0)))
        args.append(t)
    for c in consts[3:]:
        in_specs.append(_full_spec(c.shape))
        args.append(c)
    n_in = len(args)
    out_specs = [_row_spec(tm * r, n) for r, n, _ in outs]
    out_shape = [jax.ShapeDtypeStruct((T * r, n), dt) for r, n, dt in outs]
    aliases = {}
    if stacked is not None:
        slot, n_slots, stacks = stacked
        for oi, prev in stacks.items():
            r, n, dt = outs[oi]
            out_specs[oi] = pl.BlockSpec((None, tm * r, n), lambda i: (slot, i, 0))
            out_shape[oi] = jax.ShapeDtypeStruct((n_slots, T * r, n), dt)
            if prev is not None:
                aliases[len(args)] = oi
                in_specs.append(pl.BlockSpec(memory_space=pl.ANY))
                args.append(prev)
    n_alias = len(args) - n_in

    def body(*refs):
        kern(*refs[:n_in], *refs[n_in + n_alias:])

    return pl.pallas_call(
        body,
        grid=(T // tm,),
        in_specs=in_specs,
        out_specs=out_specs,
        out_shape=out_shape,
        input_output_aliases=aliases,
        compiler_params=_params("parallel"),
    )(*args)


def _rope_tables(pos, head_dim, tm):
    half = head_dim // 2
    inv = ROPE_THETA ** (-jnp.arange(half, dtype=F32) / half)
    ang = pos.astype(F32)[:, None] * inv[None, :]
    cos, sin = jnp.cos(ang), jnp.sin(ang)
    reps = LANES // head_dim
    cf = jnp.tile(jnp.concatenate([cos, cos], axis=-1), (1, reps))
    ss = jnp.tile(jnp.concatenate([-sin, sin], axis=-1), (1, reps))
    if cf.shape[0] < tm:
        cf = jnp.tile(cf, (tm // cf.shape[0], 1))
        ss = jnp.tile(ss, (tm // ss.shape[0], 1))
    return cf, ss


def _out_kernel(*refs, gated, final):
    refs = list(refs)
    o_ref = refs.pop(0)
    g_ref = refs.pop(0) if gated else None
    x_ref, w_ref = refs.pop(0), refs.pop(0)
    nf_ref = refs.pop(0) if final else None
    y_ref = refs.pop(0)
    o = o_ref[...]
    if gated:
        o = (o.astype(F32) * _silu(g_ref[...].astype(F32))).astype(BF16)
    y = x_ref[...] + _dot(o, w_ref[...])
    if final:
        ms = jnp.mean(y * y, axis=-1, keepdims=True)
        y = y * lax.rsqrt(ms + EPS) * nf_ref[...]
    y_ref[...] = y


def _out_proj(o, g, x2d, w_bf16, norm_f, tm):
    T = x2d.shape[0]
    gated, final = g is not None, norm_f is not None
    args, specs = [o], [_row_spec(tm, BRANCH)]
    if gated:
        args.append(g)
        specs.append(_row_spec(tm, BRANCH))
    args += [x2d, w_bf16]
    specs += [_row_spec(tm, D_MODEL), _full_spec(w_bf16.shape)]
    if final:
        args.append(norm_f.reshape(1, D_MODEL))
        specs.append(_full_spec((1, D_MODEL)))
    return pl.pallas_call(
        functools.partial(_out_kernel, gated=gated, final=final),
        grid=(T // tm,), in_specs=specs, out_specs=_row_spec(tm, D_MODEL),
        out_shape=jax.ShapeDtypeStruct((T, D_MODEL), F32),
        compiler_params=_params("parallel"),
    )(*args)


def _cumsum_kernel(x_ref, o_ref, *, n_blk):
    tril = _tril(PAGE).astype(F32)
    carry = jnp.zeros((1, N_HEADS), F32)
    for c in range(n_blk):
        y = _dot(tril, x_ref[0, c * PAGE:(c + 1) * PAGE, :], HIGHEST) + carry
        o_ref[0, c * PAGE:(c + 1) * PAGE, :] = y
        carry = y[PAGE - 1:PAGE, :]


def _cumsum_seq(logf):
    B, L, H = logf.shape
    return pl.pallas_call(
        functools.partial(_cumsum_kernel, n_blk=L // PAGE),
        grid=(B,),
        in_specs=[pl.BlockSpec((1, L, H), lambda b: (b, 0, 0))],
        out_specs=pl.BlockSpec((1, L, H), lambda b: (b, 0, 0)),
        out_shape=jax.ShapeDtypeStruct((B, L, H), F32),
        compiler_params=_params("parallel"),
    )(logf)


def _softmax_step(s, m, l, acc, pv_fn):
    m_new = jnp.maximum(m, jnp.max(s, axis=-1, keepdims=True))
    alpha = jnp.exp(m - m_new)
    p = jnp.exp(s - m_new)
    l = alpha * l + jnp.sum(p, axis=-1, keepdims=True)
    acc = alpha * acc + pv_fn(p.astype(BF16))
    return m_new, l, acc


def _fox_attn_kernel(q_ref, k_ref, v_ref, ck_ref, o_ref, *, tq, tk, ts):
    i = pl.program_id(2)
    n_sub = tq // ts
    qs = [q_ref[0, r * ts:(r + 1) * ts, :] for r in range(n_sub)]

    def step(j, carry, masked):
        start = pl.multiple_of(j * tk, tk)
        k = k_ref[0, pl.ds(start, tk), :]
        v = v_ref[0, pl.ds(start, tk), :]
        ck = ck_ref[0, 0, j]
        out = []
        for r in range(n_sub):
            s = _dot_nt(qs[r], k) - ck
            if masked:
                s = jnp.where(start + _iota((ts, tk), 1) <= i * tq + r * ts + _iota((ts, tk), 0), s, NEG)
            out.append(_softmax_step(s, *carry[r], lambda p: _dot(p, v)))
        return tuple(out)

    carry = tuple((jnp.full((ts, 1), NEG, F32), jnp.zeros((ts, 1), F32), jnp.zeros((ts, HEAD_DIM), F32))
                  for _ in range(n_sub))
    n_diag = tq // tk
    carry = lax.fori_loop(0, i * n_diag, lambda j, c: step(j, c, False), carry)
    for d in range(n_diag):
        carry = step(i * n_diag + d, carry, True)
    for r, (_, l, acc) in enumerate(carry):
        o_ref[0, r * ts:(r + 1) * ts, :] = (acc / l).astype(BF16)


def _fox_attn_prompt(q, k, v, cum, tq, tk, ts):
    B, L, _ = q.shape
    nb = L // tq
    ck = jnp.swapaxes(cum, 1, 2).reshape(B, N_HEADS, L // tk, 1, tk)
    return pl.pallas_call(
        functools.partial(_fox_attn_kernel, tq=tq, tk=tk, ts=ts),
        grid=(B, N_HEADS, nb),
        in_specs=[
            pl.BlockSpec((1, tq, HEAD_DIM), lambda b, h, i: (b, i, h)),
            pl.BlockSpec((1, L, HEAD_DIM), lambda b, h, i: (b, 0, h)),
            pl.BlockSpec((1, L, HEAD_DIM), lambda b, h, i: (b, 0, h)),
            pl.BlockSpec((1, 1, L // tk, 1, tk), lambda b, h, i: (b, h, 0, 0, 0)),
        ],
        out_specs=pl.BlockSpec((1, tq, HEAD_DIM), lambda b, h, i: (b, i, h)),
        out_shape=jax.ShapeDtypeStruct((B, L, BRANCH), BF16),
        compiler_params=_params("parallel", "parallel", "arbitrary"),
    )(q, k, v, ck)


def _cumsum_lanes(x):
    lane = _iota(x.shape, 1)
    d = 1
    while d < x.shape[1]:
        x = x + jnp.where(lane >= d, pltpu.roll(x, d, 1), 0.0)
        d *= 2
    return x


def _fox_dec_kernel(pt_ref, q_ref, kn_ref, vn_ref, lfn_ref, *rest, pps, n_steps, dl):
    k_refs, v_refs, lf_refs = rest[0:pps], rest[pps:2 * pps], rest[2 * pps:3 * pps]
    o_ref = rest[3 * pps]
    m_ref, l_ref, acc_ref, car_ref = rest[3 * pps + 1:]
    p = pl.program_id(1)
    nr = N_HEADS * dl
    n_chain = m_ref.shape[0]

    @pl.when(p == 0)
    def _():
        m_ref[...] = jnp.full(m_ref.shape, NEG, F32)
        l_ref[...] = jnp.zeros(l_ref.shape, F32)
        acc_ref[...] = jnp.zeros(acc_ref.shape, F32)
        car_ref[...] = jnp.zeros((N_HEADS, LANES), F32)

    qs = [q_ref[0, :, h * HEAD_DIM:(h + 1) * HEAD_DIM] for h in range(N_HEADS)]

    def attend(chains):
        off, cums = car_ref[...], {}
        for ci, (_, blocks, n_tok, _) in enumerate(chains):
            for bi, (_, _, lf) in enumerate(blocks):
                c = _cumsum_lanes(lf)
                cums[ci, bi] = (c + off)[:, 0:n_tok]
                off = off + jnp.broadcast_to(c[:, LANES - 1:LANES], (N_HEADS, LANES))
        car_ref[...] = off

        def head_rows(blocks, n_tok, which, h):
            return jnp.concatenate([blk[which][_head_rows(h, n_tok, N_HEADS), :].astype(BF16) for blk in blocks],
                                   axis=0)

        ss = []
        for ci, (_, blocks, n_tok, mask) in enumerate(chains):
            cum = jnp.concatenate([cums[ci, bi] for bi in range(len(blocks))], axis=1)
            s = jnp.concatenate([_dot_nt(qs[h], head_rows(blocks, n_tok, 0, h)) - cum[h:h + 1, :]
                                 for h in range(N_HEADS)], axis=0)
            ss.append(s if mask is None else jnp.where(mask, s, NEG))
        for (st, blocks, n_tok, _), s in zip(chains, ss):
            def pv(pb, blocks=blocks, n_tok=n_tok):
                return jnp.concatenate([_dot(pb[h * dl:(h + 1) * dl], head_rows(blocks, n_tok, 1, h))
                                        for h in range(N_HEADS)], axis=0)
            m_ref[st], l_ref[st], acc_ref[st] = _softmax_step(s, m_ref[st], l_ref[st], acc_ref[st], pv)

    n_page_chain = n_chain - 1
    per_chain = pps // n_page_chain
    page_chains = [(ci, [(k_refs[r], v_refs[r], lf_refs[r][...]) for r in range(ci * per_chain, (ci + 1) * per_chain)],
                    PAGE, None) for ci in range(n_page_chain)]
    causal = _iota((nr, dl), 1) <= _iota((nr, dl), 0) % dl
    new_chain = (n_page_chain, [(kn_ref, vn_ref, lfn_ref[0])], dl, causal)

    def finish():
        m = m_ref[0]
        for ci in range(1, n_chain):
            m = jnp.maximum(m, m_ref[ci])
        l, acc = jnp.zeros((nr, 1), F32), jnp.zeros((nr, HEAD_DIM), F32)
        for ci in range(n_chain):
            w = jnp.exp(m_ref[ci] - m)
            l, acc = l + w * l_ref[ci], acc + w * acc_ref[ci]
        out = acc / l
        for h in range(N_HEADS):
            o_ref[0, :, h * HEAD_DIM:(h + 1) * HEAD_DIM] = out[h * dl:(h + 1) * dl].astype(BF16)

    if n_steps == 1:
        attend(page_chains + [new_chain])
        finish()
    else:
        attend(page_chains)

        @pl.when(p == n_steps - 1)
        def _():
            attend([new_chain])
            finish()


def _fox_attn_decode(q, kn, vn, lfn, cache_k, cache_v, cache_lf, layer, pt_flat, n_pages, pps):
    DB, DL, _ = q.shape
    n_steps = n_pages // pps
    nr = N_HEADS * DL
    n_chain = max(pps // 4, 1) + 1
    lfn_t =jnp.pad(jnp.swapaxes(lfn, 1, 2), ((0, 0), (0, 0), (0, LANES - DL)))

    def page_spec(rows, r):
        return pl.BlockSpec((None, None, rows, LANES),
                            lambda b, p, pt, r=r: (layer, pt[b * n_pages + p * pps + r], 0, 0))

    new_rows = pl.BlockSpec((DL * N_HEADS, HEAD_DIM), lambda b, p, pt: (b, 0))
    tok = lambda n, w: pl.BlockSpec((1, n, w), lambda b, p, pt: (b, 0, 0))
    in_specs = [tok(DL, BRANCH), new_rows, new_rows, tok(N_HEADS, LANES)]
    in_specs += [page_spec(PAGE * N_HEADS, r) for r in range(pps)] * 2 + [page_spec(N_HEADS, r) for r in range(pps)]
    grid_spec = pltpu.PrefetchScalarGridSpec(
        num_scalar_prefetch=1, grid=(DB, n_steps), in_specs=in_specs, out_specs=tok(DL, BRANCH),
        scratch_shapes=[pltpu.VMEM((n_chain, nr, 1), F32), pltpu.VMEM((n_chain, nr, 1), F32),
                        pltpu.VMEM((n_chain, nr, HEAD_DIM), F32), pltpu.VMEM((N_HEADS, LANES), F32)])
    return pl.pallas_call(
        functools.partial(_fox_dec_kernel, pps=pps, n_steps=n_steps, dl=DL),
        grid_spec=grid_spec,
        out_shape=jax.ShapeDtypeStruct((DB, DL, BRANCH), BF16),
        compiler_params=_params("parallel", "arbitrary"),
    )(pt_flat, q, kn, vn, lfn_t, *([cache_k] * pps), *([cache_v] * pps), *([cache_lf] * pps))


def _key_to_float(t):
    return pltpu.bitcast(t ^ ((t >> 31) & 0x7FFFFFFF), F32)


def _kth_largest(count_ge, shape, n_sel, bits_per_step):
    def step(si, t):
        shift = 32 - bits_per_step * (si + 1)
        best = t
        for j in range(1, 2 ** bits_per_step):
            cand = t ^ lax.shift_left(jnp.int32(j), shift)
            best = jnp.where(count_ge(_key_to_float(cand)) >= n_sel, jnp.maximum(best, cand), best)
        return best
    t = lax.fori_loop(0, 32 // bits_per_step, step, jnp.full(shape, INT_MIN, jnp.int32))
    return _key_to_float(jnp.maximum(t, NEG_INF_KEY + 1))


def _strict_upper(n):
    return (_iota((n, n), 0) < _iota((n, n), 1)).astype(BF16)


def _dsa_attn_kernel(q_ref, qi_ref, wi_ref, k_ref, v_ref, ki_ref, o_ref, sc_ref, sct_ref, wb_ref, *, n_sel, tq, W):
    i = pl.program_id(1)
    nch = (i * tq + tq + W - 1) // W
    reps = W // LANES
    wi = wi_ref[0]
    for h in range(IDX_HEADS):
        wb_ref[h] = jnp.broadcast_to(wi[:, h:h + 1], (tq, LANES))
    row, col = _iota((tq, W), 0), _iota((tq, W), 1)

    def fill(c, _):
        kc = ki_ref[0, pl.ds(pl.multiple_of(c * W, W), W), :].astype(BF16)
        sc = jnp.zeros((tq, W), F32)
        for h in range(IDX_HEADS):
            d = _dot_nt(qi_ref[0, :, h * IDX_DIM:(h + 1) * IDX_DIM], kc)
            sc = sc + jnp.concatenate([wb_ref[h]] * reps, axis=1) * jnp.maximum(d, 0.0)
        sc = jnp.where(c * W + col <= i * tq + row, sc, -jnp.inf)
        sc_ref[c] = sc
        sct_ref[c] = sc.T
        return 0

    lax.fori_loop(0, nch, fill, 0)

    def count(pred):
        def body(c, acc):
            x = jnp.where(pred(sc_ref[c]), 1.0, 0.0)
            for r in range(reps):
                acc = acc + x[:, r * LANES:(r + 1) * LANES]
            return acc
        return jnp.sum(lax.fori_loop(0, nch, body, jnp.zeros((tq, LANES), F32)), axis=-1, keepdims=True)

    def count_ge_t(f):
        f8 = jnp.broadcast_to(f, (SUBLANES, tq))
        n_acc = 4

        def body(c, accs):
            accs = list(accs)
            for r in range(W // SUBLANES):
                x = jnp.where(sct_ref[c, r * SUBLANES:(r + 1) * SUBLANES, :] >= f8, 1.0, 0.0)
                accs[r % n_acc] = accs[r % n_acc] + x
            return tuple(accs)

        accs = lax.fori_loop(0, nch, body, tuple(jnp.zeros((SUBLANES, tq), F32) for _ in range(n_acc)))
        return jnp.sum((accs[0] + accs[1]) + (accs[2] + accs[3]), axis=0, keepdims=True)

    thr_row = _kth_largest(count_ge_t, (1, tq), n_sel, 1)
    eye = _iota((tq, tq), 0) == _iota((tq, tq), 1)
    thr = jnp.sum(jnp.where(eye, jnp.broadcast_to(thr_row, (tq, tq)), 0.0), axis=-1, keepdims=True)

    @pl.when(jnp.max(count(lambda s: s >= thr)) > n_sel)
    def _():
        need = n_sel - count(lambda s: s > thr)
        upper = _strict_upper(W)

        def body(c, seen):
            sc = sc_ref[c]
            eq = sc == thr
            eqf = jnp.where(eq, 1.0, 0.0)
            rank = seen + _dot(eqf.astype(BF16), upper)
            sc_ref[c] = jnp.where(eq & (rank >= need), -jnp.inf, sc)
            return seen + jnp.sum(eqf, axis=-1, keepdims=True)

        lax.fori_loop(0, nch, body, jnp.zeros((tq, 1), F32))

    qgs = [jnp.concatenate(
        [q_ref[0, :, (g * DSA_GROUP + r) * HEAD_DIM:(g * DSA_GROUP + r + 1) * HEAD_DIM] for r in range(DSA_GROUP)],
        axis=0) for g in range(DSA_KV_HEADS)]

    def body(c, carry):
        start = pl.multiple_of(c * W, W)
        bias = jnp.where(sc_ref[c] >= thr, 0.0, NEG)[None]
        out = []
        for g in range(DSA_KV_HEADS):
            m, l, acc = carry[g]
            gsl = slice(g * HEAD_DIM, (g + 1) * HEAD_DIM)
            s = _dot_nt(qgs[g], k_ref[0, pl.ds(start, W), gsl]).reshape(DSA_GROUP, tq, W) + bias
            m_new = jnp.maximum(m, jnp.max(s, axis=-1, keepdims=True))
            alpha = jnp.exp(m - m_new)
            p = jnp.exp(s - m_new)
            l = alpha * l + jnp.sum(p, axis=-1, keepdims=True)
            pv = _dot(p.reshape(DSA_GROUP * tq, W).astype(BF16), v_ref[0, pl.ds(start, W), gsl])
            out.append((m_new, l, alpha * acc + pv.reshape(DSA_GROUP, tq, HEAD_DIM)))
        return tuple(out)

    init = tuple((jnp.full((DSA_GROUP, tq, 1), NEG, F32), jnp.zeros((DSA_GROUP, tq, 1), F32),
                  jnp.zeros((DSA_GROUP, tq, HEAD_DIM), F32)) for _ in range(DSA_KV_HEADS))
    res = lax.fori_loop(0, nch, body, init)
    for g, (_, l, acc) in enumerate(res):
        out = acc / l
        for r in range(DSA_GROUP):
            hh = g * DSA_GROUP + r
            o_ref[0, :, hh * HEAD_DIM:(hh + 1) * HEAD_DIM] = out[r].astype(BF16)


def _dsa_attn_prompt(q, qi, wi, k, v, ki, n_sel, tq):
    B, L, _ = q.shape
    nb = L // tq
    W = min(4 * LANES, L)
    blk = lambda w: pl.BlockSpec((1, tq, w), lambda b, i: (b, i, 0))
    full = lambda w: pl.BlockSpec((1, L, w), lambda b, i: (b, 0, 0))
    return pl.pallas_call(
        functools.partial(_dsa_attn_kernel, n_sel=n_sel, tq=tq, W=W),
        grid=(B, nb),
        in_specs=[blk(BRANCH), blk(IDX_HEADS * IDX_DIM), blk(IDX_HEADS), full(DSA_KV), full(DSA_KV), full(IDX_DIM)],
        out_specs=blk(BRANCH),
        out_shape=jax.ShapeDtypeStruct((B, L, BRANCH), BF16),
        scratch_shapes=[pltpu.VMEM((L // W, tq, W), F32), pltpu.VMEM((L // W, W, tq), F32),
                        pltpu.VMEM((IDX_HEADS, tq, LANES), F32)],
        compiler_params=_params("parallel", "arbitrary"),
    )(q, qi, wi, k, v, ki)


def _dsa_dec_kernel(pt_ref, q_ref, qi_ref, wi_ref, kn_ref, vn_ref, kin_ref, *rest, pps, n_pages, dl, n_sel, bb):
    n_pg = bb * pps
    k_refs, v_refs, ki_refs = rest[0:n_pg], rest[n_pg:2 * n_pg], rest[2 * n_pg:3 * n_pg]
    o_ref = rest[3 * n_pg]
    ks_ref, vs_ref, kis_ref = rest[3 * n_pg + 1:]
    p = pl.program_id(1)
    past = n_pages * PAGE
    nk = past + PAGE
    nr = bb * dl
    G = DSA_KV_HEADS

    for e in range(bb):
        for r in range(pps):
            page = p * pps + r
            start = pl.multiple_of(page * (PAGE * G), PAGE * G)
            ks_ref[e, pl.ds(start, PAGE * G), :] = k_refs[e * pps + r][...]
            vs_ref[e, pl.ds(start, PAGE * G), :] = v_refs[e * pps + r][...]
            kis_ref[e, page] = ki_refs[e * pps + r][...]

    @pl.when(p == n_pages // pps - 1)
    def _():
        for e in range(bb):
            for ref, new in ((ks_ref, kn_ref), (vs_ref, vn_ref)):
                ref[e, past * G:(past + dl) * G, :] = new[e * dl * G:(e + 1) * dl * G, :]
                ref[e, (past + dl) * G:nk * G, :] = jnp.zeros(((PAGE - dl) * G, HEAD_DIM), F32)
            kis_ref[e, n_pages] = kin_ref[e]

        rows = []
        for e in range(bb):
            qi, wi = qi_ref[e], wi_ref[e]
            chunks = []
            for c in range(n_pages + 1):
                wr = wi * jnp.maximum(_dot(qi, kis_ref[e, c].astype(BF16)), 0.0)
                sc = wr[0:dl]
                for h in range(1, IDX_HEADS):
                    sc = sc + wr[h * dl:(h + 1) * dl]
                chunks.append(sc)
            rows.append(jnp.concatenate(chunks, axis=1))
        score = jnp.concatenate(rows, axis=0)
        valid = _iota((nr, nk), 1) <= past + _iota((nr, nk), 0) % dl
        score = jnp.where(valid, score, -jnp.inf)

        def count(mask):
            return jnp.sum(jnp.where(mask, 1.0, 0.0), axis=-1, keepdims=True)

        thr = _kth_largest(lambda f: count(score >= f), (nr, 1), n_sel, 2)

        def drop_surplus_ties(sc):
            need = n_sel - count(sc > thr)
            upper = _strict_upper(PAGE)
            seen, kept = jnp.zeros((nr, 1), F32), []
            for c in range(n_pages + 1):
                scc = sc[:, c * PAGE:(c + 1) * PAGE]
                eq = scc == thr
                eqf = jnp.where(eq, 1.0, 0.0)
                rank = seen + _dot(eqf.astype(BF16), upper)
                kept.append(jnp.where(eq & (rank >= need), -jnp.inf, scc))
                seen = seen + jnp.sum(eqf, axis=-1, keepdims=True)
            return jnp.concatenate(kept, axis=1)

        score = lax.cond(jnp.max(count(score >= thr)) > n_sel, drop_surplus_ties, lambda sc: sc, score)
        sel = score >= thr

        units = [(e, g) for e in range(bb) for g in range(G)]

        def stage(fn):
            return [fn(i, e, g) for i, (e, g) in enumerate(units)]

        qg = stage(lambda i, e, g: jnp.concatenate(
            [q_ref[e, :, (g * DSA_GROUP + r) * HEAD_DIM:(g * DSA_GROUP + r + 1) * HEAD_DIM].astype(F32)
             for r in range(DSA_GROUP)], axis=0).astype(BF16))
        s = stage(lambda i, e, g: jnp.where(
            sel[e * dl:(e + 1) * dl][None],
            _dot_nt(qg[i], ks_ref[e, _head_rows(g, nk, G), :].astype(BF16)).reshape(DSA_GROUP, dl, nk), NEG))
        pr = stage(lambda i, e, g: jnp.exp(s[i] - jnp.max(s[i], axis=-1, keepdims=True)))
        den = stage(lambda i, e, g: jnp.sum(pr[i], axis=-1, keepdims=True))
        pv = stage(lambda i, e, g: _dot(pr[i].reshape(DSA_GROUP * dl, nk).astype(BF16),
                                        vs_ref[e, _head_rows(g, nk, G), :].astype(BF16)))
        for i, (e, g) in enumerate(units):
            out = pv[i].reshape(DSA_GROUP, dl, HEAD_DIM) / den[i]
            for r in range(DSA_GROUP):
                hh = g * DSA_GROUP + r
                o_ref[e, :, hh * HEAD_DIM:(hh + 1) * HEAD_DIM] = out[r].astype(BF16)


def _dsa_attn_decode(q, qi, wi, kn, vn, kin, cache_k, cache_v, cache_ki, layer, pt_flat, n_pages, pps, n_sel):
    DB, DL, _ = q.shape
    nk = n_pages * PAGE + PAGE
    G = DSA_KV_HEADS
    bb = max(d for d in (1, 2, 4) if DB % d == 0)
    qi_hq = jnp.swapaxes(qi.reshape(DB, DL, IDX_HEADS, IDX_DIM), 1, 2).reshape(DB, IDX_HEADS * DL, IDX_DIM)
    wi_hq = jnp.swapaxes(wi.reshape(DB, DL, IDX_HEADS), 1, 2).reshape(DB, IDX_HEADS * DL, 1)
    kin_t = jnp.pad(jnp.swapaxes(kin, 1, 2), ((0, 0), (0, 0), (0, PAGE - DL)))

    def page_specs(rows):
        return [pl.BlockSpec((None, None, rows, LANES),
                             lambda b, p, pt, e=e, r=r: (layer, pt[(b * bb + e) * n_pages + p * pps + r], 0, 0))
                for e in range(bb) for r in range(pps)]

    tok = lambda n, w: pl.BlockSpec((bb, n, w), lambda b, p, pt: (b, 0, 0))
    new_rows = pl.BlockSpec((bb * DL * G, HEAD_DIM), lambda b, p, pt: (b, 0))
    in_specs = [tok(DL, BRANCH), tok(IDX_HEADS * DL, IDX_DIM), tok(IDX_HEADS * DL, 1),
                new_rows, new_rows, tok(IDX_DIM, PAGE)]
    in_specs += page_specs(PAGE * G) * 2 + page_specs(IDX_DIM)
    n_pg = bb * pps
    grid_spec = pltpu.PrefetchScalarGridSpec(
        num_scalar_prefetch=1, grid=(DB // bb, n_pages // pps), in_specs=in_specs, out_specs=tok(DL, BRANCH),
        scratch_shapes=[pltpu.VMEM((bb, nk * G, HEAD_DIM), F32), pltpu.VMEM((bb, nk * G, HEAD_DIM), F32),
                        pltpu.VMEM((bb, n_pages + 1, IDX_DIM, PAGE), F32)])
    return pl.pallas_call(
        functools.partial(_dsa_dec_kernel, pps=pps, n_pages=n_pages, dl=DL, n_sel=n_sel, bb=bb),
        grid_spec=grid_spec,
        out_shape=jax.ShapeDtypeStruct((DB, DL, BRANCH), BF16),
        compiler_params=_params("parallel", "arbitrary"),
    )(pt_flat, q, qi_hq, wi_hq, kn, vn, kin_t, *([cache_k] * n_pg), *([cache_v] * n_pg), *([cache_ki] * n_pg))


def _gdn_kernel(x_ref, z_ref, g_ref, b_ref, cw_ref, nw_ref, cs_ref, s0_ref, o_ref, sout_ref, xb_ref, s_ref,
                *, C, nc, bb):
    c = pl.program_id(1)
    HIST = SUBLANES

    @pl.when(c == 0)
    def _():
        xb_ref[:, 0:HIST, :] = cs_ref[...]
        s_ref[...] = s0_ref[...]

    incl, strict = _tril(C), _tril(C, strict=True)
    eye_c = (_iota((C, C), 0) == _iota((C, C), 1)).astype(F32)
    eye_h = (_iota((N_HEADS, N_HEADS), 0) == _iota((N_HEADS, N_HEADS), 1)).astype(F32)
    n_dbl = int(math.log2(C)) - 1

    convs, G_alls, Gt_alls = [], [], []
    for e in range(bb):
        xb_ref[e, HIST:HIST + C, :] = x_ref[e]
        conv = xb_ref[e, HIST:HIST + C, :] * cw_ref[CONV_W - 1:CONV_W, :]
        for j in range(CONV_W - 1):
            off = HIST - (CONV_W - 1) + j
            conv = conv + xb_ref[e, off:off + C, :] * cw_ref[j:j + 1, :]
        hist = xb_ref[e, C:C + HIST, :]
        xb_ref[e, 0:HIST, :] = hist
        convs.append(_silu(conv))
        G = _dot(incl.astype(F32), g_ref[e], HIGHEST)
        G_alls.append(G)
        Gt_alls.append(_dot_nt(eye_h, G, HIGHEST))

    units = [(e, h) for e in range(bb) for h in range(N_HEADS)]

    def stage(fn):
        return [fn(i, e, h) for i, (e, h) in enumerate(units)]

    def head(e, j, h):
        return convs[e][:, j * BRANCH + h * HEAD_DIM:j * BRANCH + (h + 1) * HEAD_DIM]

    def l2n(a):
        return a * lax.rsqrt(jnp.sum(a * a, axis=-1, keepdims=True) + EPS)

    q = stage(lambda i, e, h: l2n(head(e, 0, h)) * ATTN_SCALE)
    k = stage(lambda i, e, h: l2n(head(e, 1, h)))
    v = stage(lambda i, e, h: head(e, 2, h))
    Gc = stage(lambda i, e, h: G_alls[e][:, h:h + 1])
    bc = stage(lambda i, e, h: b_ref[e][:, h:h + 1])
    decay = stage(lambda i, e, h: jnp.exp(jnp.where(incl, Gc[i] - Gt_alls[e][h:h + 1, :], -jnp.inf)))
    kqk = stage(lambda i, e, h: _dot1(jnp.concatenate([k[i], q[i]], axis=0), k[i], _dot_nt))
    A = stage(lambda i, e, h: jnp.where(strict, bc[i] * kqk[i][0:C] * decay[i], 0.0))
    T = stage(lambda i, e, h: eye_c - A[i])
    P = stage(lambda i, e, h: _dot1(A[i], A[i]))
    for it in range(n_dbl):
        TP = stage(lambda i, e, h: _dot1(T[i], P[i]))
        T = stage(lambda i, e, h: T[i] + TP[i])
        if it + 1 < n_dbl:
            P = stage(lambda i, e, h: _dot1(P[i], P[i]))
    eG = stage(lambda i, e, h: jnp.exp(Gc[i]))
    WU = stage(lambda i, e, h: _dot1(T[i], jnp.concatenate([k[i] * (bc[i] * eG[i]), v[i] * bc[i]], axis=1)))
    Aqk = stage(lambda i, e, h: kqk[i][C:2 * C] * decay[i])
    G_last = stage(lambda i, e, h: Gc[i][C - 1:C, :])
    k_dec = stage(lambda i, e, h: k[i] * jnp.exp(G_last[i] - Gc[i]))
    S = stage(lambda i, e, h: s_ref[e, h])
    WqS = stage(lambda i, e, h: _dot1(jnp.concatenate([WU[i][:, 0:HEAD_DIM], q[i]], axis=0), S[i]))
    Vn = stage(lambda i, e, h: WU[i][:, HEAD_DIM:2 * HEAD_DIM] - WqS[i][0:C])
    o = stage(lambda i, e, h: eG[i] * WqS[i][C:2 * C] + _dot1(Aqk[i], Vn[i]))
    S_new = stage(lambda i, e, h: jnp.exp(G_last[i]) * S[i] + _dot1(k_dec[i], Vn[i], _dot_tn))
    for i, (e, h) in enumerate(units):
        sl = slice(h * HEAD_DIM, (h + 1) * HEAD_DIM)
        s_ref[e, h] = S_new[i]
        on = o[i] * lax.rsqrt(jnp.mean(o[i] * o[i], axis=-1, keepdims=True) + EPS) * nw_ref[...]
        o_ref[e, :, sl] = (on * _silu(z_ref[e, :, sl].astype(F32))).astype(BF16)

    @pl.when(c == nc - 1)
    def _():
        sout_ref[...] = s_ref[...]


def _gdn_mix(qkv, z, g, beta, conv_w, norm_w, conv_state, S0):
    B, L, _ = qkv.shape
    C = min(GDN_CHUNK, L)
    nc = L // C
    assert C & (C - 1) == 0 and L % C == 0
    bb = max(d for d in (1, 2, 4) if B % d == 0 and d * C <= 2 * GDN_CHUNK)
    cs = jnp.pad(conv_state, ((0, 0), (SUBLANES - (CONV_W - 1), 0), (0, 0)))
    blk = lambda w: pl.BlockSpec((bb, C, w), lambda b, c: (b, c, 0))
    state = pl.BlockSpec((bb, N_HEADS, HEAD_DIM, HEAD_DIM), lambda b, c: (b, 0, 0, 0))
    return pl.pallas_call(
        functools.partial(_gdn_kernel, C=C, nc=nc, bb=bb),
        grid=(B // bb, nc),
        in_specs=[blk(3 * BRANCH), blk(BRANCH), blk(N_HEADS), blk(N_HEADS),
                  pl.BlockSpec((CONV_W, 3 * BRANCH), lambda b, c: (0, 0)),
                  pl.BlockSpec((1, HEAD_DIM), lambda b, c: (0, 0)),
                  pl.BlockSpec((bb, SUBLANES, 3 * BRANCH), lambda b, c: (b, 0, 0)),
                  state],
        out_specs=[blk(BRANCH), state],
        out_shape=[jax.ShapeDtypeStruct((B, L, BRANCH), BF16),
                   jax.ShapeDtypeStruct((B, N_HEADS, HEAD_DIM, HEAD_DIM), F32)],
        scratch_shapes=[pltpu.VMEM((bb, C + SUBLANES, 3 * BRANCH), F32),
                        pltpu.VMEM((bb, N_HEADS, HEAD_DIM, HEAD_DIM), F32)],
        compiler_params=_params("parallel", "arbitrary"),
    )(qkv, z, g, beta, conv_w, norm_w.reshape(1, HEAD_DIM), cs, S0)


def _pad_cols(w, n=LANES):
    return jnp.pad(w, ((0, 0), (0, n - w.shape[1])))


def _row_tile(T):
    for tm in (512, 256):
        if T % tm == 0 and T >= 4 * tm:
            return tm
    return T


def kernel(x_prompt, x_sample, cache_fox_k, cache_fox_v, cache_fox_logf, cache_dsa_k, cache_dsa_v, cache_dsa_kidx, state_gdn_conv, state_gdn_S, page_table, norm_w, norm_f, fox_w_in, fox_b_f, fox_w_out, dsa_w_in, dsa_w_out, gdn_w_in, gdn_conv_w, gdn_A_log, gdn_dt_bias, gdn_norm_w, gdn_w_out):
    B, L, _ = x_prompt.shape
    DB, DL, _ = x_sample.shape
    n_pages = page_table.shape[1]
    past = n_pages * PAGE
    depth = norm_w.shape[0]
    pt_flat = page_table.reshape(-1)
    pages_per_step = lambda want: max(p for p in (1, 2, 4, 8, 16) if p <= want and n_pages % p == 0)
    fox_pps, dsa_pps = pages_per_step(16), pages_per_step(4)
    groups = ((B, L), (DB, DL))
    xs = [x_prompt.reshape(B * L, D_MODEL), x_sample.reshape(DB * DL, D_MODEL)]
    tms = [_row_tile(nb * nl) for nb, nl in groups]
    pos = [jnp.arange(L), past + jnp.arange(DL)]
    tq_p = max(t for t in (PAGE, 2 * PAGE, 4 * PAGE) if L % t == 0)

    pool = cache_fox_k.shape[1]
    fox_ck = cache_fox_k.reshape(-1, pool, PAGE * N_HEADS, HEAD_DIM)
    fox_cv = cache_fox_v.reshape(-1, pool, PAGE * N_HEADS, HEAD_DIM)
    fox_clf = jnp.swapaxes(cache_fox_logf, 2, 3)
    dsa_ck = cache_dsa_k.reshape(-1, pool, PAGE * DSA_KV_HEADS, HEAD_DIM)
    dsa_cv = cache_dsa_v.reshape(-1, pool, PAGE * DSA_KV_HEADS, HEAD_DIM)
    dsa_cki = jnp.swapaxes(cache_dsa_kidx, 2, 3)

    st = {name: ([], []) for name in ("fox_k", "fox_v", "fox_lf", "dsa_k", "dsa_v", "dsa_ki", "gdn_conv", "gdn_S")}
    finals = [None, None]
    n_fox = fox_w_in.shape[0]
    fox_kv_stack = [None, None]

    for i in range(depth):
        kind, j = i % 3, i // 3
        nw = norm_w[i].reshape(1, D_MODEL)
        last = i == depth - 1
        for gi, (nb, nl) in enumerate(groups):
            x2d, tm = xs[gi], tms[gi]
            T = nb * nl
            prompt = gi == 0
            sh = lambda a: a.reshape(nb, nl, a.shape[-1])
            if kind == 0:
                w = fox_w_in[j]
                w_main = jnp.concatenate([w[:, :3 * BRANCH], w[:, 3 * BRANCH + N_HEADS:]], axis=1).astype(BF16)
                w_small = _pad_cols(w[:, 3 * BRANCH:3 * BRANCH + N_HEADS]).astype(BF16)
                outs = [(1, BRANCH, BF16), (N_HEADS, HEAD_DIM, F32), (N_HEADS, HEAD_DIM, F32), (1, BRANCH, BF16),
                        (1, N_HEADS, F32)]
                stacked = None
                if prompt:
                    outs += [(1, BRANCH, BF16), (1, BRANCH, BF16)]
                    stacked = (j, n_fox, {1: fox_kv_stack[0], 2: fox_kv_stack[1]})
                res = _in_proj(_fox_in_kernel, x2d, [nw, w_main, w_small, fox_b_f[j].reshape(1, N_HEADS)], [], outs,
                               tm, stacked)
                q, k, v, g, lf = res[:5]
                if prompt:
                    o = _fox_attn_prompt(sh(q), sh(res[5]), sh(res[6]), _cumsum_seq(sh(lf)), tq_p, min(tq_p, 4 * PAGE), tq_p)
                    fox_kv_stack = [k, v]
                else:
                    o = _fox_attn_decode(sh(q), k, v, sh(lf), fox_ck, fox_cv, fox_clf, j, pt_flat, n_pages, fox_pps)
                    st["fox_k"][gi].append(k.reshape(nb, nl, N_HEADS, HEAD_DIM))
                    st["fox_v"][gi].append(v.reshape(nb, nl, N_HEADS, HEAD_DIM))
                st["fox_lf"][gi].append(lf.reshape(nb, nl, N_HEADS))
                w_out = fox_w_out[j]
            elif kind == 1:
                w = dsa_w_in[j]
                o_qi = BRANCH + 2 * DSA_KV
                o_wi = o_qi + IDX_HEADS * IDX_DIM
                o_ki = o_wi + IDX_HEADS
                o_g = o_ki + IDX_DIM
                w_main = jnp.concatenate([w[:, :o_wi], w[:, o_g:]], axis=1).astype(BF16)
                w_small = _pad_cols(jnp.concatenate([w[:, o_ki:o_g], w[:, o_wi:o_ki]], axis=1)).astype(BF16)
                tables = list(_rope_tables(pos[gi], HEAD_DIM, tm) + _rope_tables(pos[gi], IDX_DIM, tm))
                outs = [(1, BRANCH, BF16), (DSA_KV_HEADS, HEAD_DIM, F32), (DSA_KV_HEADS, HEAD_DIM, F32),
                        (1, IDX_HEADS * IDX_DIM, BF16), (1, BRANCH, BF16), (1, IDX_DIM, F32), (1, IDX_HEADS, F32)]
                if prompt:
                    outs += [(1, DSA_KV, BF16), (1, DSA_KV, BF16)]
                res = _in_proj(_dsa_in_kernel, x2d, [nw, w_main, w_small], tables, outs, tm)
                q, k, v, qi, g, ki, wi = res[:7]
                if prompt:
                    o = _dsa_attn_prompt(sh(q), sh(qi), sh(wi), sh(res[7]), sh(res[8]), sh(ki),
                                         min(TOPK_MAX, L // 4), 2 * PAGE)
                else:
                    o = _dsa_attn_decode(sh(q), sh(qi), sh(wi), k, v, sh(ki), dsa_ck, dsa_cv, dsa_cki,
                                         j, pt_flat, n_pages, dsa_pps, min(TOPK_MAX, (past + DL) // 4))
                st["dsa_k"][gi].append(k.reshape(nb, nl, DSA_KV_HEADS, HEAD_DIM))
                st["dsa_v"][gi].append(v.reshape(nb, nl, DSA_KV_HEADS, HEAD_DIM))
                st["dsa_ki"][gi].append(ki.reshape(nb, nl, IDX_DIM))
                w_out = dsa_w_out[j]
            else:
                w = gdn_w_in[j]
                w_main = jnp.concatenate([w[:, :3 * BRANCH], w[:, 3 * BRANCH + 2 * N_HEADS:]], axis=1).astype(BF16)
                w_small = _pad_cols(w[:, 3 * BRANCH:3 * BRANCH + 2 * N_HEADS]).astype(BF16)
                qkv, z, gg, beta = _in_proj(
                    _gdn_in_kernel, x2d,
                    [nw, w_main, w_small, gdn_A_log[j].reshape(1, N_HEADS), gdn_dt_bias[j].reshape(1, N_HEADS)], [],
                    [(1, 3 * BRANCH, F32), (1, BRANCH, BF16), (1, N_HEADS, F32), (1, N_HEADS, F32)], tm)
                if prompt:
                    conv0 = jnp.zeros((nb, CONV_W - 1, 3 * BRANCH), F32)
                    S0 = jnp.zeros((nb, N_HEADS, HEAD_DIM, HEAD_DIM), F32)
                else:
                    conv0, S0 = state_gdn_conv[j], state_gdn_S[j]
                o, S_new = _gdn_mix(sh(qkv), sh(z), sh(gg), sh(beta), gdn_conv_w[j], gdn_norm_w[j], conv0, S0)
                tail = jnp.concatenate([conv0, sh(qkv)[:, -min(nl, CONV_W - 1):]], axis=1)
                st["gdn_conv"][gi].append(tail[:, -(CONV_W - 1):])
                st["gdn_S"][gi].append(S_new)
                g = None
                w_out = gdn_w_out[j]
            y = _out_proj(o.reshape(T, BRANCH), g, x2d, w_out.astype(BF16), norm_f if last else None, tm)
            if last:
                finals[gi] = y.reshape(nb, nl, D_MODEL)
            else:
                xs[gi] = y

    stk = lambda name, gi: jnp.stack(st[name][gi], axis=0)
    fox_kv_p = [a.reshape(n_fox, B, L, N_HEADS, HEAD_DIM) for a in fox_kv_stack]
    return (finals[0], finals[1],
            fox_kv_p[0], fox_kv_p[1], stk("fox_lf", 0),
            stk("fox_k", 1), stk("fox_v", 1), stk("fox_lf", 1),
            stk("dsa_k", 0), stk("dsa_v", 0), stk("dsa_ki", 0),
            stk("dsa_k", 1), stk("dsa_v", 1), stk("dsa_ki", 1),
            stk("gdn_conv", 0), stk("gdn_S", 0),
            stk("gdn_conv", 1), stk("gdn_S", 1))
```
